```python
import math
import jax, jax.numpy as jnp
from jax import lax
import numpy as np

D_MODEL = 1024
BATCH = 8
SEQ = 4096
DEPTH = 2

D_MIX = D_MODEL
HEAD_DIM = 64
ATT_WIDTH = 3 * D_MIX // 8
ATT_HEADS = ATT_WIDTH // HEAD_DIM
LRU_WIDTH = 3 * D_MIX // 8
LRU_BLOCKS = 6
LRU_BLOCK_DIM = LRU_WIDTH // LRU_BLOCKS
S5_WIDTH = D_MIX - ATT_WIDTH - LRU_WIDTH
S5_GROUP_CH = 16
S5_GROUPS = S5_WIDTH // S5_GROUP_CH
S5_STATE = 64
CONV_WIDTH = 4
LRU_C = 8.0
MOBA_BLOCK = 256
MOBA_TOPK = 3
Q_CHUNK = 64
ROPE_THETA = 10000.0
D_FF = 2816
N_EXPERTS = 8
TOP_K = 2
D_FF_EXPERT = 3584
N_DENSE = (DEPTH + 1) // 2
N_MOE = DEPTH // 2
IN_WIDTH = 3 * ATT_WIDTH + 2 * LRU_WIDTH + S5_WIDTH
IN_SPLITS = (ATT_WIDTH, 2 * ATT_WIDTH, 3 * ATT_WIDTH,
             3 * ATT_WIDTH + LRU_WIDTH, 3 * ATT_WIDTH + 2 * LRU_WIDTH)
EPS = 1e-6
NEG = -1e30

kernel_name = 'hybrid_moba_rglru_s5_moe_block'


def rms_norm(x, g):
    x32 = x.astype(jnp.float32)
    y = x32 * lax.rsqrt(jnp.mean(x32 * x32, axis=-1, keepdims=True) + EPS)
    return (y * g.astype(jnp.float32)).astype(x.dtype)


def rope_tables(positions):
    inv = ROPE_THETA ** (-jnp.arange(0, HEAD_DIM, 2, dtype=jnp.float32) / HEAD_DIM)
    ang = positions.astype(jnp.float32)[..., None] * inv
    return jnp.cos(ang)[:, :, None, :], jnp.sin(ang)[:, :, None, :]


def apply_rope(x, cos, sin):
    half = x.shape[-1] // 2
    x32 = x.astype(jnp.float32)
    x1, x2 = x32[..., :half], x32[..., half:]
    return jnp.concatenate([x1 * cos - x2 * sin, x2 * cos + x1 * sin], axis=-1).astype(x.dtype)


def moba_attention(q, k, v):
    b, s, h, d = q.shape
    n_blk = -(-s // MOBA_BLOCK)
    s_pad = n_blk * MOBA_BLOCK
    top_k = min(MOBA_TOPK, n_blk)
    q = jnp.transpose(q, (0, 2, 1, 3)) * (d ** -0.5)
    pad = ((0, 0), (0, 0), (0, s_pad - s), (0, 0))
    k = jnp.pad(jnp.transpose(k, (0, 2, 1, 3)), pad)
    v = jnp.pad(jnp.transpose(v, (0, 2, 1, 3)), pad)
    k_blocks = k.reshape(b, h, n_blk, MOBA_BLOCK, d)
    v_blocks = v.reshape(b, h, n_blk, MOBA_BLOCK, d)
    k_mean = jnp.mean(k_blocks.astype(jnp.float32), axis=3)
    gather = jax.vmap(jax.vmap(lambda blocks, idx: blocks[idx]))
    blk_ids = jnp.arange(n_blk)
    q_off = jnp.arange(Q_CHUNK)
    k_off = jnp.arange(MOBA_BLOCK)

    def chunk(ci):
        s0 = ci * Q_CHUNK
        j = s0 // MOBA_BLOCK
        qc = lax.dynamic_slice_in_dim(q, s0, Q_CHUNK, axis=2)
        gate = jnp.einsum('bhqd,bhnd->bhqn', qc.astype(jnp.float32), k_mean)
        gate = jnp.where(blk_ids < j, gate, NEG)
        _, sel = lax.top_k(gate, top_k)
        sel_valid = jnp.arange(top_k) < j
        k_sel = gather(k_blocks, sel)
        v_sel = gather(v_blocks, sel)
        k_own = lax.dynamic_slice_in_dim(k, j * MOBA_BLOCK, MOBA_BLOCK, axis=2)
        v_own = lax.dynamic_slice_in_dim(v, j * MOBA_BLOCK, MOBA_BLOCK, axis=2)
        s_sel = jnp.einsum('bhqd,bhqnkd->bhqnk', qc, k_sel).astype(jnp.float32)
        s_sel = jnp.where(sel_valid[:, None], s_sel, NEG).reshape(b, h, Q_CHUNK, top_k * MOBA_BLOCK)
        s_own = jnp.einsum('bhqd,bhkd->bhqk', qc, k_own).astype(jnp.float32)
        causal = (j * MOBA_BLOCK + k_off)[None, :] <= (s0 + q_off)[:, None]
        s_own = jnp.where(causal, s_own, NEG)
        p = jax.nn.softmax(jnp.concatenate([s_own, s_sel], axis=-1), axis=-1).astype(v.dtype)
        p_own = p[..., :MOBA_BLOCK]
        p_sel = p[..., MOBA_BLOCK:].reshape(b, h, Q_CHUNK, top_k, MOBA_BLOCK)
        return (jnp.einsum('bhqk,bhkd->bhqd', p_own, v_own)
                + jnp.einsum('bhqnk,bhqnkd->bhqd', p_sel, v_sel))

    out = lax.map(chunk, jnp.arange(s // Q_CHUNK))
    return jnp.transpose(out, (1, 0, 3, 2, 4)).reshape(b, s, h * d)


def linear_scan(a, u):
    def combine(e1, e2):
        return (e1[0] * e2[0], e2[0] * e1[1] + e2[1])
    return lax.associative_scan(combine, (a, u), axis=1)[1]


def rglru_branch(xr, gate, conv_w, conv_b, w_a, b_a, w_x, b_x, lam):
    b, s, w = xr.shape
    xc = lax.conv_general_dilated(xr, conv_w[:, None, :], window_strides=(1,),
                                  padding=[(CONV_WIDTH - 1, 0)],
                                  dimension_numbers=('NWC', 'WIO', 'NWC'),
                                  feature_group_count=w) + conv_b
    xb = xc.reshape(b, s, LRU_BLOCKS, LRU_BLOCK_DIM)
    r = jax.nn.sigmoid(jnp.einsum('bshi,hij->bshj', xb, w_a).reshape(b, s, w) + b_a)
    i = jax.nn.sigmoid(jnp.einsum('bshi,hij->bshj', xb, w_x).reshape(b, s, w) + b_x)
    log_a = -LRU_C * r.astype(jnp.float32) * jax.nn.softplus(-lam.astype(jnp.float32))
    a = jnp.exp(log_a)
    u = jnp.sqrt(jnp.maximum(-jnp.expm1(2.0 * log_a), 0.0)) * (i * xc).astype(jnp.float32)
    hseq = linear_scan(a, u)
    return hseq.astype(xr.dtype) * jax.nn.gelu(gate)


def s5_branch(u, lam_re, lam_im, log_dt, b_re, b_im, c_re, c_im, d_skip, glu_w, glu_b):
    b, s, w = u.shape
    f32 = jnp.float32
    ug = u.reshape(b, s, S5_GROUPS, S5_GROUP_CH).astype(f32)
    dt = jnp.exp(log_dt.astype(f32))[:, None]
    lr, li = lam_re.astype(f32), lam_im.astype(f32)
    mag = jnp.exp(lr * dt)
    ab_re, ab_im = mag * jnp.cos(li * dt), mag * jnp.sin(li * dt)
    den = lr * lr + li * li
    nr, ni = ab_re - 1.0, ab_im
    co_re = (nr * lr + ni * li) / den
    co_im = (ni * lr - nr * li) / den
    br, bi = b_re.astype(f32), b_im.astype(f32)
    bb_re = co_re[..., None] * br - co_im[..., None] * bi
    bb_im = co_re[..., None] * bi + co_im[..., None] * br
    bu_re = jnp.einsum('bsgh,gph->bsgp', ug, bb_re)
    bu_im = jnp.einsum('bsgh,gph->bsgp', ug, bb_im)
    a_re = jnp.broadcast_to(ab_re, bu_re.shape)
    a_im = jnp.broadcast_to(ab_im, bu_re.shape)

    def combine(e1, e2):
        ar1, ai1, xr1, xi1 = e1
        ar2, ai2, xr2, xi2 = e2
        return (ar2 * ar1 - ai2 * ai1, ar2 * ai1 + ai2 * ar1,
                ar2 * xr1 - ai2 * xi1 + xr2, ar2 * xi1 + ai2 * xr1 + xi2)

    _, _, st_re, st_im = lax.associative_scan(combine, (a_re, a_im, bu_re, bu_im), axis=1)
    y = (jnp.einsum('bsgp,ghp->bsgh', st_re, c_re.astype(f32))
         - jnp.einsum('bsgp,ghp->bsgh', st_im, c_im.astype(f32))
         + d_skip.astype(f32).reshape(S5_GROUPS, S5_GROUP_CH) * ug)
    y = jax.nn.gelu(y.reshape(b, s, w)).astype(u.dtype)
    return y * jax.nn.sigmoid(y @ glu_w + glu_b)


def mixer(h, cos, sin, w_in, conv_w, conv_b, w_a, b_a, w_x, b_x, lam,
          lam_re, lam_im, log_dt, s5b_re, s5b_im, s5c_re, s5c_im, s5_d, glu_w, glu_b,
          mix_gain, w_out):
    b, s, _ = h.shape
    proj = h @ w_in
    q, k, v, xr, gate, u = jnp.split(proj, IN_SPLITS, axis=-1)
    q = apply_rope(q.reshape(b, s, ATT_HEADS, HEAD_DIM), cos, sin)
    k = apply_rope(k.reshape(b, s, ATT_HEADS, HEAD_DIM), cos, sin)
    v = v.reshape(b, s, ATT_HEADS, HEAD_DIM)
    o_att = moba_attention(q, k, v)
    o_lru = rglru_branch(xr, gate, conv_w, conv_b, w_a, b_a, w_x, b_x, lam)
    o_s5 = s5_branch(u, lam_re, lam_im, log_dt, s5b_re, s5b_im, s5c_re, s5c_im, s5_d, glu_w, glu_b)
    o = jnp.concatenate([
        rms_norm(o_att, mix_gain[:ATT_WIDTH]),
        rms_norm(o_lru, mix_gain[ATT_WIDTH:ATT_WIDTH + LRU_WIDTH]),
        rms_norm(o_s5, mix_gain[ATT_WIDTH + LRU_WIDTH:])], axis=-1)
    return o @ w_out


def swiglu(h, wg, wu, wd):
    return (jax.nn.silu(h @ wg) * (h @ wu)) @ wd


def moe_swiglu(h, router_w, wg, wu, wd):
    logits = (h @ router_w).astype(jnp.float32)
    top_v, top_i = lax.top_k(logits, TOP_K)
    probs = jax.nn.softmax(top_v, axis=-1)
    weights = jnp.sum(jax.nn.one_hot(top_i, N_EXPERTS, dtype=jnp.float32) * probs[..., None], axis=-2)
    out = jnp.zeros_like(h)
    for e in range(N_EXPERTS):
        out = out + weights[..., e:e + 1].astype(h.dtype) * swiglu(h, wg[e], wu[e], wd[e])
    return out


def setup_inputs(seed: int = 0) -> dict:
    key = jax.random.key(seed)
    keys = jax.random.split(key, 48)
    counter = [0]
    f32 = jnp.float32
    L = DEPTH

    def nxt():
        kk = keys[counter[0]]
        counter[0] += 1
        return kk

    def nrm(shape, scale):
        return jax.random.normal(nxt(), shape, f32) * scale

    x = nrm((BATCH, SEQ, D_MODEL), 1.0)
    c = nrm((BATCH, D_MODEL), 1.0)
    offs = jax.random.randint(nxt(), (BATCH, 1), 0, 2048, dtype=jnp.int32)
    positions = offs + jnp.arange(SEQ, dtype=jnp.int32)[None, :]
    w_in = nrm((L, D_MODEL, IN_WIDTH), D_MODEL ** -0.5)
    lru_conv_w = nrm((L, CONV_WIDTH, LRU_WIDTH), CONV_WIDTH ** -0.5)
    lru_conv_b = nrm((L, LRU_WIDTH), 0.01)
    lru_w_a = nrm((L, LRU_BLOCKS, LRU_BLOCK_DIM, LRU_BLOCK_DIM), LRU_BLOCK_DIM ** -0.5)
    lru_b_a = nrm((L, LRU_WIDTH), 0.01)
    lru_w_x = nrm((L, LRU_BLOCKS, LRU_BLOCK_DIM, LRU_BLOCK_DIM), LRU_BLOCK_DIM ** -0.5)
    lru_b_x = nrm((L, LRU_WIDTH), 0.01)
    a_c = jax.random.uniform(nxt(), (L, LRU_WIDTH), f32, 0.9, 0.999)
    a0 = a_c ** (1.0 / LRU_C)
    lru_lambda = jnp.log(a0) - jnp.log1p(-a0)
    s5_lambda_re = -0.5 + nrm((L, S5_GROUPS, S5_STATE), 0.01)
    s5_lambda_im = math.pi * jnp.arange(S5_STATE, dtype=f32) + nrm((L, S5_GROUPS, S5_STATE), 0.01)
    s5_log_dt = jax.random.uniform(nxt(), (L, S5_GROUPS), f32, math.log(1e-3), math.log(1e-1))
    s5_b_re = nrm((L, S5_GROUPS, S5_STATE, S5_GROUP_CH), (2.0 * S5_GROUP_CH) ** -0.5)
    s5_b_im = nrm((L, S5_GROUPS, S5_STATE, S5_GROUP_CH), (2.0 * S5_GROUP_CH) ** -0.5)
    s5_c_re = nrm((L, S5_GROUPS, S5_GROUP_CH, S5_STATE), (2.0 * S5_STATE) ** -0.5)
    s5_c_im = nrm((L, S5_GROUPS, S5_GROUP_CH, S5_STATE), (2.0 * S5_STATE) ** -0.5)
    s5_d = nrm((L, S5_WIDTH), 1.0)
    s5_glu_w = nrm((L, S5_WIDTH, S5_WIDTH), S5_WIDTH ** -0.5)
    s5_glu_b = nrm((L, S5_WIDTH), 0.01)
    mix_gain = 1.0 + nrm((L, D_MIX), 0.01)
    w_out = nrm((L, D_MIX, D_MODEL), D_MIX ** -0.5)
    norm1_g = 1.0 + nrm((L, D_MODEL), 0.01)
    norm2_g = 1.0 + nrm((L, D_MODEL), 0.01)
    ada_w = nrm((L, D_MODEL, 6 * D_MODEL), 0.5 * D_MODEL ** -0.5)
    ada_b = nrm((L, 6 * D_MODEL), 0.01)
    ffn_w_gate = nrm((N_DENSE, D_MODEL, D_FF), D_MODEL ** -0.5)
    ffn_w_up = nrm((N_DENSE, D_MODEL, D_FF), D_MODEL ** -0.5)
    ffn_w_down = nrm((N_DENSE, D_FF, D_MODEL), D_FF ** -0.5)
    router_w = nrm((N_MOE, D_MODEL, N_EXPERTS), D_MODEL ** -0.5)
    moe_w_gate = nrm((N_MOE, N_EXPERTS, D_MODEL, D_FF_EXPERT), D_MODEL ** -0.5)
    moe_w_up = nrm((N_MOE, N_EXPERTS, D_MODEL, D_FF_EXPERT), D_MODEL ** -0.5)
    moe_w_down = nrm((N_MOE, N_EXPERTS, D_FF_EXPERT, D_MODEL), D_FF_EXPERT ** -0.5)
    final_g = 1.0 + nrm((D_MODEL,), 0.01)
    return {'x': x, 'c': c, 'positions': positions, 'w_in': w_in,
            'lru_conv_w': lru_conv_w, 'lru_conv_b': lru_conv_b, 'lru_w_a': lru_w_a,
            'lru_b_a': lru_b_a, 'lru_w_x': lru_w_x, 'lru_b_x': lru_b_x, 'lru_lambda': lru_lambda,
            's5_lambda_re': s5_lambda_re, 's5_lambda_im': s5_lambda_im, 's5_log_dt': s5_log_dt,
            's5_b_re': s5_b_re, 's5_b_im': s5_b_im, 's5_c_re': s5_c_re, 's5_c_im': s5_c_im,
            's5_d': s5_d, 's5_glu_w': s5_glu_w, 's5_glu_b': s5_glu_b, 'mix_gain': mix_gain,
            'w_out': w_out, 'norm1_g': norm1_g, 'norm2_g': norm2_g, 'ada_w': ada_w, 'ada_b': ada_b,
            'ffn_w_gate': ffn_w_gate, 'ffn_w_up': ffn_w_up, 'ffn_w_down': ffn_w_down,
            'router_w': router_w, 'moe_w_gate': moe_w_gate, 'moe_w_up': moe_w_up,
            'moe_w_down': moe_w_down, 'final_g': final_g}


def reference(x, c, positions, w_in, lru_conv_w, lru_conv_b, lru_w_a, lru_b_a, lru_w_x, lru_b_x,
              lru_lambda, s5_lambda_re, s5_lambda_im, s5_log_dt, s5_b_re, s5_b_im, s5_c_re, s5_c_im,
              s5_d, s5_glu_w, s5_glu_b, mix_gain, w_out, norm1_g, norm2_g, ada_w, ada_b,
              ffn_w_gate, ffn_w_up, ffn_w_down, router_w, moe_w_gate, moe_w_up, moe_w_down, final_g):
    cos, sin = rope_tables(positions)
    cond = jax.nn.silu(c)
    for l in range(DEPTH):
        mod = (cond @ ada_w[l] + ada_b[l])[:, None, :]
        shift1, scale1, gate1, shift2, scale2, gate2 = jnp.split(mod, 6, axis=-1)
        h = rms_norm(x, norm1_g[l]) * (1.0 + scale1) + shift1
        x = x + gate1 * mixer(h, cos, sin, w_in[l], lru_conv_w[l], lru_conv_b[l], lru_w_a[l],
                              lru_b_a[l], lru_w_x[l], lru_b_x[l], lru_lambda[l],
                              s5_lambda_re[l], s5_lambda_im[l], s5_log_dt[l], s5_b_re[l],
                              s5_b_im[l], s5_c_re[l], s5_c_im[l], s5_d[l], s5_glu_w[l],
                              s5_glu_b[l], mix_gain[l], w_out[l])
        h = rms_norm(x, norm2_g[l]) * (1.0 + scale2) + shift2
        if l % 2 == 0:
            f = swiglu(h, ffn_w_gate[l // 2], ffn_w_up[l // 2], ffn_w_down[l // 2])
        else:
            f = moe_swiglu(h, router_w[l // 2], moe_w_gate[l // 2], moe_w_up[l // 2], moe_w_down[l // 2])
        x = x + gate2 * f
    return rms_norm(x, final_g)
```

```python
import functools
import math

import jax
import jax.numpy as jnp
from jax import lax
from jax.experimental import pallas as pl
from jax.experimental.pallas import tpu as pltpu

F32 = jnp.float32
BF16 = jnp.bfloat16

HEAD_DIM = 64
ATT_WIDTH = 384
LRU_WIDTH = 384
LRU_BLOCKS = 6
S5_WIDTH = 256
S5_GROUPS = 16
S5_GROUP_CH = 16
S5_STATE = 64
S5_NS = S5_GROUPS * S5_STATE
LRU_C = 8.0
MOBA_BLOCK = 256
MOBA_TOPK = 3
N_EXPERTS = 8
EPS = 1e-6
NEG = -1e30
LANES = 128
VMEM_LIMIT = 48 * 1024 * 1024

TOK_TILE = 512
SCAN_TILE = 256
MOE_TILE = 512


def _params(*sem):
    return pltpu.CompilerParams(dimension_semantics=sem, vmem_limit_bytes=VMEM_LIMIT)


def _dot(a, b):
    return jnp.dot(a, b, preferred_element_type=F32)


def _dot_t(a, b):
    return lax.dot_general(a, b, (((1,), (1,)), ((), ())), preferred_element_type=F32)


def _sigmoid(x):
    return 1.0 / (1.0 + jnp.exp(-x))


def _gelu(x):
    c = math.sqrt(2.0 / math.pi)
    return 0.5 * x * (1.0 + jnp.tanh(c * (x + 0.044715 * (x * x * x))))


def _rms(x, g):
    ms = jnp.mean(x * x, axis=-1, keepdims=True)
    return (x * lax.rsqrt(ms + EPS)) * g


def _mod_kernel(c_ref, w_ref, b_ref, o_ref):
    c = c_ref[...]
    cond = (c * _sigmoid(c)).astype(BF16)
    o_ref[...] = _dot(cond, w_ref[...].astype(BF16)) + b_ref[...]


def _adaln_mod(c, ada_w, ada_b):
    n_layers, d, n = ada_w.shape
    b = c.shape[0]
    nc = n // 4
    return pl.pallas_call(
        _mod_kernel,
        grid=(n_layers, n // nc),
        in_specs=[pl.BlockSpec((b, d), lambda l, j: (0, 0)),
                  pl.BlockSpec((None, d, nc), lambda l, j: (l, 0, j)),
                  pl.BlockSpec((None, 1, nc), lambda l, j: (l, 0, j))],
        out_specs=pl.BlockSpec((None, b, nc), lambda l, j: (l, 0, j)),
        out_shape=jax.ShapeDtypeStruct((n_layers, b, n), F32),
        compiler_params=_params("arbitrary", "arbitrary"),
        name="adaln_mod",
    )(c, ada_w, ada_b.reshape(n_layers, 1, n))


def _inproj_kernel(x_ref, mod_ref, g_ref, w_ref, cos_ref, sin_ref,
                   q_ref, k_ref, v_ref, xr_ref, gt_ref, u_ref):
    mod = mod_ref[...]
    h = _rms(x_ref[...], g_ref[...]) * (1.0 + mod[1:2]) + mod[0:1]
    proj = _dot(h.astype(BF16), w_ref[...])
    cos = cos_ref[...]
    sin = sin_ref[...]
    lane = lax.broadcasted_iota(jnp.int32, cos.shape, 1)
    first_half = (lane % HEAD_DIM) < (HEAD_DIM // 2)

    def rope(t):
        outs = []
        for i in range(ATT_WIDTH // LANES):
            xi = t[:, LANES * i:LANES * (i + 1)]
            partner = jnp.where(first_half,
                                pltpu.roll(xi, LANES - HEAD_DIM // 2, 1),
                                pltpu.roll(xi, HEAD_DIM // 2, 1))
            outs.append(xi * cos + partner * sin)
        return jnp.concatenate(outs, axis=1)

    a = ATT_WIDTH
    q_ref[...] = (rope(proj[:, 0:a]) * (HEAD_DIM ** -0.5)).astype(BF16)
    k_ref[...] = rope(proj[:, a:2 * a]).astype(BF16)
    v_ref[...] = proj[:, 2 * a:3 * a].astype(BF16)
    xr_ref[...] = proj[:, 3 * a:3 * a + LRU_WIDTH]
    gt_ref[...] = proj[:, 3 * a + LRU_WIDTH:3 * a + 2 * LRU_WIDTH]
    u_ref[...] = proj[:, 3 * a + 2 * LRU_WIDTH:]


def _inproj(x, mod, g, w_in_bf16, cos_t, sin_t):
    b, s, d = x.shape
    tt = min(TOK_TILE, s)
    n_in = w_in_bf16.shape[1]
    tok = lambda w: pl.BlockSpec((None, tt, w), lambda i, j: (i, j, 0))
    full = lambda r, c: pl.BlockSpec((r, c), lambda i, j: (0, 0))
    outs = [jax.ShapeDtypeStruct((b, s, ATT_WIDTH), BF16)] * 3 + [
        jax.ShapeDtypeStruct((b, s, LRU_WIDTH), F32),
        jax.ShapeDtypeStruct((b, s, LRU_WIDTH), F32),
        jax.ShapeDtypeStruct((b, s, S5_WIDTH), F32)]
    return pl.pallas_call(
        _inproj_kernel,
        grid=(b, s // tt),
        in_specs=[tok(d), pl.BlockSpec((None, 6, d), lambda i, j: (i, 0, 0)), full(1, d),
                  full(d, n_in), tok(LANES), tok(LANES)],
        out_specs=[tok(ATT_WIDTH)] * 3 + [tok(LRU_WIDTH), tok(LRU_WIDTH), tok(S5_WIDTH)],
        out_shape=outs,
        compiler_params=_params("arbitrary", "arbitrary"),
        name="inproj",
    )(x, mod, g.reshape(1, d), w_in_bf16, cos_t, sin_t)


def _attn_kernel(q_ref, k_ref, v_ref, o_ref, kmean_ref, m_ref, l_ref, acc_ref, *, n_blk):
    j = pl.program_id(2)
    blk = MOBA_BLOCK

    @pl.when(j == 0)
    def _():
        for n in range(n_blk):
            kb = k_ref[n * blk:(n + 1) * blk, :].astype(F32)
            kmean_ref[n:n + 1, :] = jnp.mean(kb, axis=0, keepdims=True)

    q = q_ref[...]
    lane = lax.broadcasted_iota(jnp.int32, q.shape, 1)
    zero = jnp.zeros_like(q)
    q2 = jnp.concatenate([jnp.where(lane < HEAD_DIM, q, zero),
                          jnp.where(lane >= HEAD_DIM, q, zero)], axis=0)
    rows = 2 * blk

    kmean = kmean_ref[...]
    km_hi = kmean.astype(BF16)
    km_lo = (kmean - km_hi.astype(F32)).astype(BF16)
    gate = _dot_t(q2, km_hi) + _dot_t(q2, km_lo)
    col = lax.broadcasted_iota(jnp.int32, gate.shape, 1)
    valid = col < j
    gw = jnp.where(valid, gate, NEG)
    sel = jnp.zeros(gate.shape, dtype=jnp.bool_)
    for _ in range(MOBA_TOPK):
        mx = jnp.max(gw, axis=1, keepdims=True)
        idx = jnp.min(jnp.where(gw == mx, col, n_blk), axis=1, keepdims=True)
        pick = col == idx
        sel = jnp.logical_or(sel, pick)
        gw = jnp.where(pick, -jnp.inf, gw)
    sel_f = jnp.where(jnp.logical_and(sel, valid), 1.0, 0.0).astype(F32)

    own = pl.multiple_of(j * blk, blk)
    s = _dot_t(q2, k_ref[pl.ds(own, blk), :])
    qpos = lax.broadcasted_iota(jnp.int32, s.shape, 0) % blk
    kpos = lax.broadcasted_iota(jnp.int32, s.shape, 1)
    s = jnp.where(kpos <= qpos, s, NEG)
    m0 = jnp.max(s, axis=1, keepdims=True)
    p = jnp.exp(s - m0)
    m_ref[...] = m0
    l_ref[...] = jnp.sum(p, axis=1, keepdims=True)
    acc_ref[...] = _dot(p.astype(BF16), v_ref[pl.ds(own, blk), :])

    def body(n, carry):
        start = pl.multiple_of(n * blk, blk)
        sn = _dot_t(q2, k_ref[pl.ds(start, blk), :])
        seln = jnp.sum(jnp.where(col == n, sel_f, 0.0), axis=1, keepdims=True)
        sn = jnp.where(seln > 0.5, sn, NEG)
        m_old = m_ref[...]
        m_new = jnp.maximum(m_old, jnp.max(sn, axis=1, keepdims=True))
        alpha = jnp.exp(m_old - m_new)
        pn = jnp.exp(sn - m_new)
        l_ref[...] = alpha * l_ref[...] + jnp.sum(pn, axis=1, keepdims=True)
        acc_ref[...] = alpha * acc_ref[...] + _dot(pn.astype(BF16), v_ref[pl.ds(start, blk), :])
        m_ref[...] = m_new
        return carry

    lax.fori_loop(0, j, body, 0)
    out = acc_ref[...] / l_ref[...]
    o_ref[...] = jnp.where(lane < HEAD_DIM, out[:blk], out[blk:])


def _attention(q, k, v):
    b, s, w = q.shape
    n_blk = s // MOBA_BLOCK
    n_pair = w // LANES
    rows = 2 * MOBA_BLOCK
    return pl.pallas_call(
        functools.partial(_attn_kernel, n_blk=n_blk),
        grid=(b, n_pair, n_blk),
        in_specs=[pl.BlockSpec((None, MOBA_BLOCK, LANES), lambda i, p, j: (i, j, p)),
                  pl.BlockSpec((None, s, LANES), lambda i, p, j: (i, 0, p)),
                  pl.BlockSpec((None, s, LANES), lambda i, p, j: (i, 0, p))],
        out_specs=pl.BlockSpec((None, MOBA_BLOCK, LANES), lambda i, p, j: (i, j, p)),
        out_shape=jax.ShapeDtypeStruct((b, s, w), F32),
        scratch_shapes=[pltpu.VMEM((n_blk, LANES), F32),
                        pltpu.VMEM((rows, 1), F32),
                        pltpu.VMEM((rows, 1), F32),
                        pltpu.VMEM((rows, LANES), F32)],
        compiler_params=_params("arbitrary", "arbitrary", "arbitrary"),
        name="moba_attention",
    )(q, k, v)


def _shift_rows(x, d, fill):
    row = lax.broadcasted_iota(jnp.int32, x.shape, 0)
    return jnp.where(row >= d, pltpu.roll(x, d, 0), fill)


def _lru_kernel(xr_ref, gt_ref, cw_ref, cb_ref, wab_ref, bab_ref, sp_ref, o_ref, hist_ref, h_ref):
    lt = xr_ref.shape[0]
    w = LRU_WIDTH

    @pl.when(pl.program_id(1) == 0)
    def _():
        hist_ref[0:8, :] = jnp.zeros((8, w), F32)
        h_ref[...] = jnp.zeros_like(h_ref)

    xr = xr_ref[...]
    hist_ref[8:8 + lt, :] = xr
    cw = cw_ref[...]
    n_tap = cw.shape[0]
    xc = cb_ref[...] + cw[n_tap - 1:n_tap] * xr
    for back in range(1, n_tap):
        xc = xc + cw[n_tap - 1 - back:n_tap - back] * hist_ref[8 - back:8 - back + lt, :]
    hist_ref[0:8, :] = xr[lt - 8:lt]

    ri = _sigmoid(_dot(xc.astype(BF16), wab_ref[...]) + bab_ref[...])
    r = ri[:, :w]
    i = ri[:, w:]
    log_a = (-LRU_C) * r * sp_ref[...]
    a = jnp.exp(log_a)
    x2 = 2.0 * log_a
    expm1 = jnp.tanh(0.5 * x2) * (jnp.exp(x2) + 1.0)
    u = jnp.sqrt(jnp.maximum(-expm1, 0.0)) * (i * xc)

    d = 1
    while d < lt:
        u = u + a * _shift_rows(u, d, 0.0)
        a = a * _shift_rows(a, d, 1.0)
        d *= 2
    h = u + a * h_ref[...]
    h_ref[...] = h[lt - 1:lt]
    o_ref[...] = h * _gelu(gt_ref[...])


def _lru(xr, gate, conv_w, conv_b, wab_bf16, bab, sp):
    b, s, w = xr.shape
    lt = min(SCAN_TILE, s)
    tok = pl.BlockSpec((None, lt, w), lambda i, j: (i, j, 0))
    full = lambda a: pl.BlockSpec(a.shape, lambda i, j: (0,) * a.ndim)
    return pl.pallas_call(
        _lru_kernel,
        grid=(b, s // lt),
        in_specs=[tok, tok, full(conv_w), full(conv_b), full(wab_bf16), full(bab), full(sp)],
        out_specs=tok,
        out_shape=jax.ShapeDtypeStruct((b, s, w), F32),
        scratch_shapes=[pltpu.VMEM((8 + lt, w), F32), pltpu.VMEM((1, w), F32)],
        compiler_params=_params("arbitrary", "arbitrary"),
        name="rglru",
    )(xr, gate, conv_w, conv_b, wab_bf16, bab, sp)


def _s5_kernel(u_ref, wb_ref, apow_ref, wc_ref, d_ref, gw_ref, gb_ref, o_ref, xs_ref):
    lt = u_ref.shape[0]
    ns = S5_NS

    @pl.when(pl.program_id(1) == 0)
    def _():
        xs_ref[...] = jnp.zeros_like(xs_ref)

    u = u_ref[...]
    bu = _dot(u.astype(BF16), wb_ref[...])
    re = bu[:, :ns]
    im = bu[:, ns:]
    d = 1
    while d < lt:
        ar = apow_ref[0, d - 1:d, :]
        ai = apow_ref[1, d - 1:d, :]
        rs = _shift_rows(re, d, 0.0)
        isf = _shift_rows(im, d, 0.0)
        re, im = re + ar * rs - ai * isf, im + ar * isf + ai * rs
        d *= 2
    xr0 = xs_ref[0:1, :]
    xi0 = xs_ref[1:2, :]
    acr = apow_ref[0]
    aci = apow_ref[1]
    re, im = re + acr * xr0 - aci * xi0, im + acr * xi0 + aci * xr0
    xs_ref[0:1, :] = re[lt - 1:lt]
    xs_ref[1:2, :] = im[lt - 1:lt]
    x = jnp.concatenate([re, im], axis=1).astype(BF16)
    y = _gelu(_dot(x, wc_ref[...]) + d_ref[...] * u)
    z = _sigmoid(_dot(y.astype(BF16), gw_ref[...]) + gb_ref[...])
    o_ref[...] = y * z


def _s5(u, wb_bf16, apow, wc_bf16, d_skip, glu_w_bf16, glu_b):
    b, s, w = u.shape
    lt = apow.shape[1]
    tok = pl.BlockSpec((None, lt, w), lambda i, j: (i, j, 0))
    full = lambda a: pl.BlockSpec(a.shape, lambda i, j: (0,) * a.ndim)
    return pl.pallas_call(
        _s5_kernel,
        grid=(b, s // lt),
        in_specs=[tok, full(wb_bf16), full(apow), full(wc_bf16), full(d_skip), full(glu_w_bf16), full(glu_b)],
        out_specs=tok,
        out_shape=jax.ShapeDtypeStruct((b, s, w), F32),
        scratch_shapes=[pltpu.VMEM((2, S5_NS), F32)],
        compiler_params=_params("arbitrary", "arbitrary"),
        name="s5",
    )(u, wb_bf16, apow, wc_bf16, d_skip, glu_w_bf16, glu_b)


def _s5_tables(lam_re, lam_im, log_dt, b_re, b_im, c_re, c_im, lt):
    g, p, h = b_re.shape
    dt = jnp.exp(log_dt)[:, None]
    mag = jnp.exp(lam_re * dt)
    ab_re, ab_im = mag * jnp.cos(lam_im * dt), mag * jnp.sin(lam_im * dt)
    den = lam_re * lam_re + lam_im * lam_im
    nr, ni = ab_re - 1.0, ab_im
    co_re = (nr * lam_re + ni * lam_im) / den
    co_im = (ni * lam_re - nr * lam_im) / den
    bb_re = co_re[..., None] * b_re - co_im[..., None] * b_im
    bb_im = co_re[..., None] * b_im + co_im[..., None] * b_re
    eye = jnp.eye(g, dtype=F32)
    wb = jnp.concatenate([jnp.einsum('gph,gk->ghkp', bb_re, eye).reshape(g * h, g * p),
                          jnp.einsum('gph,gk->ghkp', bb_im, eye).reshape(g * h, g * p)], axis=1)
    wc = jnp.concatenate([jnp.einsum('ghp,gk->kpgh', c_re, eye).reshape(g * p, g * h),
                          -jnp.einsum('ghp,gk->kpgh', c_im, eye).reshape(g * p, g * h)], axis=0)

    def cmul(x, y):
        return (x[0] * y[0] - x[1] * y[1], x[0] * y[1] + x[1] * y[0])

    ar = jnp.broadcast_to(ab_re.reshape(1, g * p), (lt, g * p))
    ai = jnp.broadcast_to(ab_im.reshape(1, g * p), (lt, g * p))
    pr, pi = lax.associative_scan(cmul, (ar, ai), axis=0)
    return wb.astype(BF16), jnp.stack([pr, pi]), wc.astype(BF16)


def _outproj_kernel(oa_ref, ol_ref, os_ref, x_ref, mod_ref, mg_ref, w_ref, g2_ref, *rest, moe):
    if moe:
        rw_ref, x_out, h_out, ti_out, tw_out = rest
    else:
        x_out, h_out = rest
    mod = mod_ref[...]
    mg = mg_ref[...]
    a, l = ATT_WIDTH, LRU_WIDTH
    w = w_ref[...]
    o = (_dot(_rms(oa_ref[...], mg[:, :a]).astype(BF16), w[:a])
         + _dot(_rms(ol_ref[...], mg[:, a:a + l]).astype(BF16), w[a:a + l])
         + _dot(_rms(os_ref[...], mg[:, a + l:]).astype(BF16), w[a + l:]))
    x = x_ref[...] + mod[2:3] * o
    x_out[...] = x
    h = _rms(x, g2_ref[...]) * (1.0 + mod[4:5]) + mod[3:4]
    h_out[...] = h.astype(h_out.dtype)
    if moe:
        hb = h.astype(BF16)
        h_lo = (h - hb.astype(F32)).astype(BF16)
        rw = rw_ref[...]
        rw_hi = rw.astype(BF16)
        rw_lo = (rw - rw_hi.astype(F32)).astype(BF16)
        logits = _dot(hb, rw_hi) + _dot(hb, rw_lo) + _dot(h_lo, rw_hi)
        col = lax.broadcasted_iota(jnp.int32, logits.shape, 1)
        m1 = jnp.max(logits, axis=1, keepdims=True)
        i1 = jnp.min(jnp.where(logits == m1, col, N_EXPERTS), axis=1, keepdims=True)
        rest_l = jnp.where(col == i1, -jnp.inf, logits)
        m2 = jnp.max(rest_l, axis=1, keepdims=True)
        i2 = jnp.min(jnp.where(rest_l == m2, col, N_EXPERTS), axis=1, keepdims=True)
        e2 = jnp.exp(m2 - m1)
        den = 1.0 + e2
        ti_out[...] = jnp.concatenate([i1, i2], axis=1)
        tw_out[...] = jnp.concatenate([1.0 / den, e2 / den], axis=1)


def _outproj(o_att, o_lru, o_s5, x, mod, mix_gain, w_out_bf16, g2, router_w=None):
    b, s, d = x.shape
    tt = min(TOK_TILE, s)
    moe = router_w is not None
    tok = lambda w: pl.BlockSpec((None, tt, w), lambda i, j: (i, j, 0))
    full = lambda r, c: pl.BlockSpec((r, c), lambda i, j: (0, 0))
    in_specs = [tok(ATT_WIDTH), tok(LRU_WIDTH), tok(S5_WIDTH), tok(d),
                pl.BlockSpec((None, 6, d), lambda i, j: (i, 0, 0)), full(1, d), full(d, d), full(1, d)]
    args = [o_att, o_lru, o_s5, x, mod, mix_gain.reshape(1, d), w_out_bf16, g2.reshape(1, d)]
    out_specs = [tok(d), tok(d)]
    out_shape = [jax.ShapeDtypeStruct((b, s, d), F32),
                 jax.ShapeDtypeStruct((b, s, d), F32 if moe else BF16)]
    if moe:
        in_specs.append(full(d, N_EXPERTS))
        args.append(router_w)
        out_specs += [tok(2), tok(2)]
        out_shape += [jax.ShapeDtypeStruct((b, s, 2), jnp.int32), jax.ShapeDtypeStruct((b, s, 2), F32)]
    return pl.pallas_call(
        functools.partial(_outproj_kernel, moe=moe),
        grid=(b, s // tt),
        in_specs=in_specs, out_specs=out_specs, out_shape=out_shape,
        compiler_params=_params("arbitrary", "arbitrary"),
        name="outproj_moe" if moe else "outproj",
    )(*args)


def _ffn_kernel(h_ref, x_ref, mod_ref, wg_ref, wu_ref, wd_ref, o_ref, acc_ref):
    f = pl.program_id(2)
    h = h_ref[...]
    g = _dot(h, wg_ref[...])
    u = _dot(h, wu_ref[...])
    part = _dot(((g * _sigmoid(g)) * u).astype(BF16), wd_ref[...])

    @pl.when(f == 0)
    def _():
        acc_ref[...] = part

    @pl.when(f > 0)
    def _():
        acc_ref[...] += part

    @pl.when(f == pl.num_programs(2) - 1)
    def _():
        o_ref[...] = x_ref[...] + mod_ref[...][5:6] * acc_ref[...]


def _ffn(h, x, mod, wg, wu, wd):
    b, s, d = x.shape
    tt = min(TOK_TILE, s)
    dff = wg.shape[1]
    n_f = 2
    fc = dff // n_f
    tok = lambda w: pl.BlockSpec((None, tt, w), lambda i, j, f: (i, j, 0))
    return pl.pallas_call(
        _ffn_kernel,
        grid=(b, s // tt, n_f),
        in_specs=[tok(d), tok(d), pl.BlockSpec((None, 6, d), lambda i, j, f: (i, 0, 0)),
                  pl.BlockSpec((d, fc), lambda i, j, f: (0, f)),
                  pl.BlockSpec((d, fc), lambda i, j, f: (0, f)),
                  pl.BlockSpec((fc, d), lambda i, j, f: (f, 0))],
        out_specs=tok(d),
        out_shape=jax.ShapeDtypeStruct((b, s, d), F32),
        scratch_shapes=[pltpu.VMEM((tt, d), F32)],
        compiler_params=_params("arbitrary", "arbitrary", "arbitrary"),
        name="ffn_dense",
    )(h, x, mod, wg, wu, wd)


def _final_kernel(x_ref, g_ref, o_ref):
    o_ref[...] = _rms(x_ref[...], g_ref[...])


def _final_norm(x, g):
    b, s, d = x.shape
    tt = min(TOK_TILE, s)
    tok = pl.BlockSpec((None, tt, d), lambda i, j: (i, j, 0))
    return pl.pallas_call(
        _final_kernel, grid=(b, s // tt),
        in_specs=[tok, pl.BlockSpec((1, d), lambda i, j: (0, 0))],
        out_specs=tok, out_shape=jax.ShapeDtypeStruct((b, s, d), F32),
        compiler_params=_params("arbitrary", "arbitrary"),
        name="final_norm",
    )(x, g.reshape(1, d))


def _rope_tables(positions):
    half = HEAD_DIM // 2
    inv = 10000.0 ** (-jnp.arange(0, HEAD_DIM, 2, dtype=F32) / HEAD_DIM)
    ang = positions.astype(F32)[..., None] * inv
    cos, sin = jnp.cos(ang), jnp.sin(ang)
    reps = LANES // HEAD_DIM
    cos_t = jnp.tile(jnp.concatenate([cos, cos], axis=-1), (1, 1, reps))
    sin_t = jnp.tile(jnp.concatenate([-sin, sin], axis=-1), (1, 1, reps))
    return cos_t, sin_t


def _block_diag(w):
    h, i, j = w.shape
    eye = jnp.eye(h, dtype=w.dtype)
    return jnp.einsum('hij,hk->hikj', w, eye).reshape(h * i, h * j)


def _moe_kernel(te_ref, na_ref, src_ref, dst_ref, h_hbm, rw_ref, wg_ref, wu_ref, wd_ref, y_hbm,
                xbuf, ybuf, acc_ref, gsem, ssem):
    i = pl.program_id(0)
    f = pl.program_id(1)
    tm = xbuf.shape[0]
    active = i < na_ref[0]

    @pl.when(jnp.logical_and(i == 0, f == 0))
    def _():
        xbuf[...] = jnp.zeros_like(xbuf)

    def row_in(r, t):
        return pltpu.make_async_copy(h_hbm.at[pl.ds(t, 1), :], xbuf.at[pl.ds(r, 1), :], gsem)

    def row_out(r, t):
        return pltpu.make_async_copy(ybuf.at[pl.ds(r, 1), :], y_hbm.at[pl.ds(t, 1), :], ssem)

    @pl.when(jnp.logical_and(active, f == 0))
    def _():
        def start(r, c):
            t = src_ref[0, 0, r]

            @pl.when(t >= 0)
            def _():
                row_in(r, t).start()
            return c
        lax.fori_loop(0, tm, start, 0)

        def wait(r, c):
            t = src_ref[0, 0, r]

            @pl.when(t >= 0)
            def _():
                row_in(r, t).wait()
            return c
        lax.fori_loop(0, tm, wait, 0)

    @pl.when(active)
    def _():
        x = xbuf[...].astype(BF16)
        g = _dot(x, wg_ref[...])
        u = _dot(x, wu_ref[...])
        part = _dot(((g * _sigmoid(g)) * u).astype(BF16), wd_ref[...])

        @pl.when(f == 0)
        def _():
            acc_ref[...] = part

        @pl.when(f > 0)
        def _():
            acc_ref[...] += part

    @pl.when(jnp.logical_and(active, f == pl.num_programs(1) - 1))
    def _():
        ybuf[...] = acc_ref[...] * rw_ref[...]

        def start(r, c):
            t = dst_ref[0, 0, r]

            @pl.when(t >= 0)
            def _():
                row_out(r, t).start()
            return c
        lax.fori_loop(0, tm, start, 0)

        def wait(r, c):
            t = dst_ref[0, 0, r]

            @pl.when(t >= 0)
            def _():
                row_out(r, t).wait()
            return c
        lax.fori_loop(0, tm, wait, 0)


def _moe_experts(h2d, tile_expert, n_active, src, dst, rw, wg, wu, wd):
    t, d = h2d.shape
    n_pad = src.shape[0]
    tm = MOE_TILE
    n_tiles = n_pad // tm
    dff = wg.shape[2]
    n_f = 2
    fc = dff // n_f
    grid_spec = pltpu.PrefetchScalarGridSpec(
        num_scalar_prefetch=2,
        grid=(n_tiles, n_f),
        in_specs=[pl.BlockSpec((1, 1, tm), lambda i, f, te, na: (i, 0, 0), memory_space=pltpu.SMEM),
                  pl.BlockSpec((1, 1, tm), lambda i, f, te, na: (i, 0, 0), memory_space=pltpu.SMEM),
                  pl.BlockSpec(memory_space=pl.ANY),
                  pl.BlockSpec((tm, 1), lambda i, f, te, na: (i, 0)),
                  pl.BlockSpec((None, d, fc), lambda i, f, te, na: (te[i], 0, f)),
                  pl.BlockSpec((None, d, fc), lambda i, f, te, na: (te[i], 0, f)),
                  pl.BlockSpec((None, fc, d), lambda i, f, te, na: (te[i], f, 0))],
        out_specs=pl.BlockSpec(memory_space=pl.ANY),
        scratch_shapes=[pltpu.VMEM((tm, d), F32), pltpu.VMEM((tm, d), F32), pltpu.VMEM((tm, d), F32),
                        pltpu.SemaphoreType.DMA(()), pltpu.SemaphoreType.DMA(())])
    return pl.pallas_call(
        _moe_kernel,
        grid_spec=grid_spec,
        out_shape=jax.ShapeDtypeStruct((2 * t, d), F32),
        compiler_params=_params("arbitrary", "arbitrary"),
        name="moe_experts",
    )(tile_expert, n_active, src.reshape(n_tiles, 1, tm), dst.reshape(n_tiles, 1, tm), h2d,
      rw.reshape(n_pad, 1), wg, wu, wd)


def _moe_plan(ti, tw):
    t = ti.shape[0]
    tm = MOE_TILE
    n_tiles = (2 * t) // tm + N_EXPERTS
    n_pad = n_tiles * tm
    onehot = jnp.any(ti[:, :, None] == jnp.arange(N_EXPERTS, dtype=jnp.int32), axis=1)
    pos = jnp.cumsum(onehot.astype(jnp.int32), axis=0) - 1
    counts = pos[-1] + 1
    padded = ((counts + tm - 1) // tm) * tm
    ends = jnp.cumsum(padded)
    starts = ends - padded
    dest = jnp.take_along_axis(starts[None, :] + pos, ti, axis=1).reshape(-1)
    tok = jnp.repeat(jnp.arange(t, dtype=jnp.int32), 2)
    slot = jnp.tile(jnp.arange(2, dtype=jnp.int32), t)
    src = jnp.full((n_pad,), -1, jnp.int32).at[dest].set(tok)
    dst = jnp.full((n_pad,), -1, jnp.int32).at[dest].set(slot * t + tok)
    rw = jnp.zeros((n_pad,), F32).at[dest].set(tw.reshape(-1))
    tile_start = jnp.arange(n_tiles, dtype=jnp.int32) * tm
    tile_expert = jnp.minimum(jnp.sum((tile_start[:, None] >= ends[None, :]).astype(jnp.int32), axis=1),
                              N_EXPERTS - 1)
    n_active = (ends[-1] // tm).astype(jnp.int32).reshape(1)
    return tile_expert, n_active, src, dst, rw


def _combine_kernel(x_ref, y_ref, mod_ref, o_ref):
    y = y_ref[...]
    o_ref[...] = x_ref[...] + mod_ref[...][5:6] * (y[0] + y[1])


def _combine(x, y2, mod):
    b, s, d = x.shape
    tt = min(TOK_TILE, s)
    tok = pl.BlockSpec((None, tt, d), lambda i, j: (i, j, 0))
    return pl.pallas_call(
        _combine_kernel, grid=(b, s // tt),
        in_specs=[tok, pl.BlockSpec((2, None, tt, d), lambda i, j: (0, i, j, 0)),
                  pl.BlockSpec((None, 6, d), lambda i, j: (i, 0, 0))],
        out_specs=tok, out_shape=jax.ShapeDtypeStruct((b, s, d), F32),
        compiler_params=_params("arbitrary", "arbitrary"),
        name="moe_combine",
    )(x, y2, mod)


def kernel(x, c, positions, w_in, lru_conv_w, lru_conv_b, lru_w_a, lru_b_a, lru_w_x, lru_b_x, lru_lambda,
           s5_lambda_re, s5_lambda_im, s5_log_dt, s5_b_re, s5_b_im, s5_c_re, s5_c_im, s5_d, s5_glu_w,
           s5_glu_b, mix_gain, w_out, norm1_g, norm2_g, ada_w, ada_b, ffn_w_gate, ffn_w_up, ffn_w_down,
           router_w, moe_w_gate, moe_w_up, moe_w_down, final_g):
    b, s, d = x.shape
    depth = w_in.shape[0]
    cos_t, sin_t = _rope_tables(positions)
    mods = _adaln_mod(c, ada_w, ada_b).reshape(depth, b, 6, d)
    lt = min(SCAN_TILE, s)
    for l in range(depth):
        mod = mods[l]
        q, k, v, xr, gate, u = _inproj(x, mod, norm1_g[l], w_in[l].astype(BF16), cos_t, sin_t)
        o_att = _attention(q, k, v)
        wab = jnp.concatenate([_block_diag(lru_w_a[l]), _block_diag(lru_w_x[l])], axis=1).astype(BF16)
        bab = jnp.concatenate([lru_b_a[l], lru_b_x[l]]).reshape(1, -1)
        sp = jax.nn.softplus(-lru_lambda[l]).reshape(1, -1)
        o_lru = _lru(xr, gate, lru_conv_w[l], lru_conv_b[l].reshape(1, -1), wab, bab, sp)
        wb, apow, wc = _s5_tables(s5_lambda_re[l], s5_lambda_im[l], s5_log_dt[l], s5_b_re[l], s5_b_im[l],
                                  s5_c_re[l], s5_c_im[l], lt)
        o_s5 = _s5(u, wb, apow, wc, s5_d[l].reshape(1, -1), s5_glu_w[l].astype(BF16), s5_glu_b[l].reshape(1, -1))
        if l % 2 == 0:
            x, h = _outproj(o_att, o_lru, o_s5, x, mod, mix_gain[l], w_out[l].astype(BF16), norm2_g[l])
            e = l // 2
            x = _ffn(h, x, mod, ffn_w_gate[e].astype(BF16), ffn_w_up[e].astype(BF16), ffn_w_down[e].astype(BF16))
        else:
            x, h, ti, tw = _outproj(o_att, o_lru, o_s5, x, mod, mix_gain[l], w_out[l].astype(BF16), norm2_g[l],
                                    router_w[l // 2])
            e = l // 2
            plan = _moe_plan(ti.reshape(b * s, 2), tw.reshape(b * s, 2))
            y2 = _moe_experts(h.reshape(b * s, d), *plan, moe_w_gate[e].astype(BF16), moe_w_up[e].astype(BF16),
                              moe_w_down[e].astype(BF16))
            x = _combine(x, y2.reshape(2, b, s, d), mod)
    return _final_norm(x, final_g)
```

```python
import functools
import math

import jax
import jax.numpy as jnp
from jax import lax
from jax.experimental import pallas as pl
from jax.experimental.pallas import tpu as pltpu

F32 = jnp.float32
BF16 = jnp.bfloat16

HEAD_DIM = 64
ATT_WIDTH = 384
LRU_WIDTH = 384
LRU_BLOCKS = 6
S5_WIDTH = 256
S5_GROUPS = 16
S5_GROUP_CH = 16
S5_STATE = 64
S5_NS = S5_GROUPS * S5_STATE
LRU_C = 8.0
MOBA_BLOCK = 256
MOBA_TOPK = 3
N_EXPERTS = 8
EPS = 1e-6
NEG = -1e30
LANES = 128
VMEM_LIMIT = 48 * 1024 * 1024
ATTN_VMEM_LIMIT = 56 * 1024 * 1024

TOK_TILE = 512
SCAN_TILE = 256
MOE_TILE = 512


def _params(*sem):
    return pltpu.CompilerParams(dimension_semantics=sem, vmem_limit_bytes=VMEM_LIMIT)


def _dot(a, b):
    return jnp.dot(a, b, preferred_element_type=F32)


def _dot_t(a, b):
    return lax.dot_general(a, b, (((1,), (1,)), ((), ())), preferred_element_type=F32)


def _sigmoid(x):
    return 1.0 / (1.0 + jnp.exp(-x))


def _gelu(x):
    c = math.sqrt(2.0 / math.pi)
    return 0.5 * x * (1.0 + jnp.tanh(c * (x + 0.044715 * (x * x * x))))


def _rms(x, g):
    ms = jnp.mean(x * x, axis=-1, keepdims=True)
    return (x * lax.rsqrt(ms + EPS)) * g


def _mod_kernel(c_ref, w_ref, b_ref, o_ref):
    c = c_ref[...]
    cond = (c * _sigmoid(c)).astype(BF16)
    o_ref[...] = _dot(cond, w_ref[...].astype(BF16)) + b_ref[...]


def _adaln_mod(c, ada_w, ada_b):
    n_layers, d, n = ada_w.shape
    b = c.shape[0]
    nc = n // 4
    return pl.pallas_call(
        _mod_kernel,
        grid=(n_layers, n // nc),
        in_specs=[pl.BlockSpec((b, d), lambda l, j: (0, 0)),
                  pl.BlockSpec((None, d, nc), lambda l, j: (l, 0, j)),
                  pl.BlockSpec((None, 1, nc), lambda l, j: (l, 0, j))],
        out_specs=pl.BlockSpec((None, b, nc), lambda l, j: (l, 0, j)),
        out_shape=jax.ShapeDtypeStruct((n_layers, b, n), F32),
        compiler_params=_params("arbitrary", "arbitrary"),
        name="adaln_mod",
    )(c, ada_w, ada_b.reshape(n_layers, 1, n))


def _inproj_kernel(x_ref, mod_ref, g_ref, w_ref, cos_ref, sin_ref,
                   q_ref, k_ref, v_ref, xr_ref, gt_ref, u_ref):
    mod = mod_ref[...]
    h = _rms(x_ref[...], g_ref[...]) * (1.0 + mod[1:2]) + mod[0:1]
    proj = _dot(h.astype(BF16), w_ref[...])
    cos = cos_ref[...]
    sin = sin_ref[...]
    lane = lax.broadcasted_iota(jnp.int32, cos.shape, 1)
    first_half = (lane % HEAD_DIM) < (HEAD_DIM // 2)

    def rope(t):
        outs = []
        for i in range(ATT_WIDTH // LANES):
            xi = t[:, LANES * i:LANES * (i + 1)]
            partner = jnp.where(first_half,
                                pltpu.roll(xi, LANES - HEAD_DIM // 2, 1),
                                pltpu.roll(xi, HEAD_DIM // 2, 1))
            outs.append(xi * cos + partner * sin)
        return jnp.concatenate(outs, axis=1)

    a = ATT_WIDTH
    q_ref[...] = (rope(proj[:, 0:a]) * (HEAD_DIM ** -0.5)).astype(BF16)
    k = rope(proj[:, a:2 * a]).astype(BF16)
    v = proj[:, 2 * a:3 * a].astype(BF16)
    tt = k.shape[0]
    blk_id = (pl.program_id(1) * tt + lax.broadcasted_iota(jnp.int32, (tt, LANES), 0)) // MOBA_BLOCK
    onehot = jnp.where(blk_id == lane, 1.0, 0.0).astype(BF16)
    ones = jnp.ones((tt, LANES), BF16)
    for i in range(a // LANES):
        k_ref[:, 2 * LANES * i:2 * LANES * i + LANES] = k[:, LANES * i:LANES * (i + 1)]
        k_ref[:, 2 * LANES * i + LANES:2 * LANES * (i + 1)] = onehot
        v_ref[:, 2 * LANES * i:2 * LANES * i + LANES] = v[:, LANES * i:LANES * (i + 1)]
        v_ref[:, 2 * LANES * i + LANES:2 * LANES * (i + 1)] = ones
    xr_ref[...] = proj[:, 3 * a:3 * a + LRU_WIDTH]
    gt_ref[...] = proj[:, 3 * a + LRU_WIDTH:3 * a + 2 * LRU_WIDTH]
    u_ref[...] = proj[:, 3 * a + 2 * LRU_WIDTH:]


def _inproj(x, mod, g, w_in_bf16, cos_t, sin_t):
    b, s, d = x.shape
    tt = min(TOK_TILE, s)
    n_in = w_in_bf16.shape[1]
    tok = lambda w: pl.BlockSpec((None, tt, w), lambda i, j: (i, j, 0))
    full = lambda r, c: pl.BlockSpec((r, c), lambda i, j: (0, 0))
    outs = [jax.ShapeDtypeStruct((b, s, ATT_WIDTH), BF16)] + [jax.ShapeDtypeStruct((b, s, 2 * ATT_WIDTH), BF16)] * 2 + [
        jax.ShapeDtypeStruct((b, s, LRU_WIDTH), F32),
        jax.ShapeDtypeStruct((b, s, LRU_WIDTH), F32),
        jax.ShapeDtypeStruct((b, s, S5_WIDTH), F32)]
    return pl.pallas_call(
        _inproj_kernel,
        grid=(b, s // tt),
        in_specs=[tok(d), pl.BlockSpec((None, 6, d), lambda i, j: (i, 0, 0)), full(1, d),
                  full(d, n_in), tok(LANES), tok(LANES)],
        out_specs=[tok(ATT_WIDTH), tok(2 * ATT_WIDTH), tok(2 * ATT_WIDTH),
                   tok(LRU_WIDTH), tok(LRU_WIDTH), tok(S5_WIDTH)],
        out_shape=outs,
        compiler_params=_params("arbitrary", "arbitrary"),
        name="inproj",
    )(x, mod, g.reshape(1, d), w_in_bf16, cos_t, sin_t)


def _attn_kernel(q_ref, k_ref, v_ref, o_ref, kmean_ref, s_ref, mrun_ref, acc_ref, *, n_blk, n_pair):
    j = pl.program_id(1)
    blk = MOBA_BLOCK
    rows = 2 * blk
    kw = 2 * LANES

    @pl.when(j == 0)
    def _():
        for n in range(n_blk):
            for p in range(n_pair):
                kb = k_ref[n * blk:(n + 1) * blk, kw * p:kw * p + LANES]
                kmean_ref[n:n + 1, LANES * p:LANES * (p + 1)] = jnp.mean(kb.astype(F32), axis=0, keepdims=True)

    lane = lax.broadcasted_iota(jnp.int32, (blk, LANES), 1)
    bid = lax.broadcasted_iota(jnp.int32, (n_blk, rows), 0)
    valid = bid < j
    place = jnp.where(lax.broadcasted_iota(jnp.int32, (n_blk, LANES), 0)
                      == lax.broadcasted_iota(jnp.int32, (n_blk, LANES), 1), 1.0, 0.0).astype(BF16)
    own = pl.multiple_of(j * blk, blk)
    qpos = lax.broadcasted_iota(jnp.int32, (rows, blk), 0) % blk
    kpos = lax.broadcasted_iota(jnp.int32, (rows, blk), 1)
    causal = kpos <= qpos

    q_aug = []
    s_own = []
    for p in range(n_pair):
        q = q_ref[:, LANES * p:LANES * (p + 1)]
        zero = jnp.zeros_like(q)
        q2 = jnp.concatenate([jnp.where(lane < HEAD_DIM, q, zero),
                              jnp.where(lane >= HEAD_DIM, q, zero)], axis=0)
        kmean = kmean_ref[:, LANES * p:LANES * (p + 1)]
        km_hi = kmean.astype(BF16)
        km_lo = (kmean - km_hi.astype(F32)).astype(BF16)
        gw = jnp.where(valid, _dot_t(km_hi, q2) + _dot_t(km_lo, q2), NEG)
        sel = jnp.zeros(gw.shape, dtype=jnp.bool_)
        for _ in range(MOBA_TOPK):
            mx = jnp.max(gw, axis=0, keepdims=True)
            idx = jnp.min(jnp.where(gw == mx, bid, n_blk), axis=0, keepdims=True)
            pick = bid == idx
            sel = jnp.logical_or(sel, pick)
            gw = jnp.where(pick, -jnp.inf, gw)
        keep = jnp.logical_or(jnp.logical_and(sel, valid), bid == j)
        bias_t = jnp.where(keep, 0.0, NEG).astype(BF16)
        bias = lax.dot_general(bias_t, place, (((0,), (0,)), ((), ())), preferred_element_type=F32)
        qa = jnp.concatenate([q2, bias.astype(BF16)], axis=1)
        q_aug.append(qa)
        so = jnp.where(causal, _dot_t(qa, k_ref[pl.ds(own, blk), kw * p:kw * (p + 1)]), NEG)
        s_own.append(so)
        mrun_ref[p] = so

    def scores(n, carry):
        start = pl.multiple_of(n * blk, blk)
        for p in range(n_pair):
            sn = _dot_t(q_aug[p], k_ref[pl.ds(start, blk), kw * p:kw * (p + 1)])
            s_ref[p, n] = sn
            mrun_ref[p] = jnp.maximum(mrun_ref[p], sn)
        return carry

    lax.fori_loop(0, j, scores, 0)

    m = []
    for p in range(n_pair):
        mp = jnp.max(mrun_ref[p], axis=1, keepdims=True)
        m.append(mp)
        acc_ref[p] = _dot(jnp.exp(s_own[p] - mp).astype(BF16), v_ref[pl.ds(own, blk), kw * p:kw * (p + 1)])

    def values(n, carry):
        start = pl.multiple_of(n * blk, blk)
        for p in range(n_pair):
            pn = jnp.exp(s_ref[p, n] - m[p]).astype(BF16)
            acc_ref[p] += _dot(pn, v_ref[pl.ds(start, blk), kw * p:kw * (p + 1)])
        return carry

    lax.fori_loop(0, j, values, 0)
    for p in range(n_pair):
        acc = acc_ref[p]
        out = acc[:, :LANES] / acc[:, LANES:]
        o_ref[:, LANES * p:LANES * (p + 1)] = jnp.where(lane < HEAD_DIM, out[:blk], out[blk:])


def _attention(q, kaug, vaug):
    b, s, w = q.shape
    n_blk = s // MOBA_BLOCK
    n_pair = w // LANES
    rows = 2 * MOBA_BLOCK
    once = pl.Buffered(1)
    return pl.pallas_call(
        functools.partial(_attn_kernel, n_blk=n_blk, n_pair=n_pair),
        grid=(b, n_blk),
        in_specs=[pl.BlockSpec((None, MOBA_BLOCK, w), lambda i, j: (i, j, 0)),
                  pl.BlockSpec((None, s, 2 * w), lambda i, j: (i, 0, 0), pipeline_mode=once),
                  pl.BlockSpec((None, s, 2 * w), lambda i, j: (i, 0, 0), pipeline_mode=once)],
        out_specs=pl.BlockSpec((None, MOBA_BLOCK, w), lambda i, j: (i, j, 0)),
        out_shape=jax.ShapeDtypeStruct((b, s, w), F32),
        scratch_shapes=[pltpu.VMEM((n_blk, w), F32),
                        pltpu.VMEM((n_pair, n_blk, rows, MOBA_BLOCK), F32),
                        pltpu.VMEM((n_pair, rows, MOBA_BLOCK), F32),
                        pltpu.VMEM((n_pair, rows, 2 * LANES), F32)],
        compiler_params=pltpu.CompilerParams(dimension_semantics=("arbitrary", "arbitrary"),
                                             vmem_limit_bytes=ATTN_VMEM_LIMIT),
        name="moba_attention",
    )(q, kaug, vaug)


def _shift_rows(x, d, fill):
    row = lax.broadcasted_iota(jnp.int32, x.shape, 0)
    return jnp.where(row >= d, pltpu.roll(x, d, 0), fill)


def _lru_kernel(xr_ref, gt_ref, cw_ref, cb_ref, wab_ref, bab_ref, sp_ref, o_ref, hist_ref, h_ref):
    lt = xr_ref.shape[0]
    w = LRU_WIDTH

    @pl.when(pl.program_id(1) == 0)
    def _():
        hist_ref[0:8, :] = jnp.zeros((8, w), F32)
        h_ref[...] = jnp.zeros_like(h_ref)

    xr = xr_ref[...]
    hist_ref[8:8 + lt, :] = xr
    cw = cw_ref[...]
    n_tap = cw.shape[0]
    xc = cb_ref[...] + cw[n_tap - 1:n_tap] * xr
    for back in range(1, n_tap):
        xc = xc + cw[n_tap - 1 - back:n_tap - back] * hist_ref[8 - back:8 - back + lt, :]
    hist_ref[0:8, :] = xr[lt - 8:lt]

    ri = _sigmoid(_dot(xc.astype(BF16), wab_ref[...]) + bab_ref[...])
    r = ri[:, :w]
    i = ri[:, w:]
    log_a = (-LRU_C) * r * sp_ref[...]
    a = jnp.exp(log_a)
    x2 = 2.0 * log_a
    expm1 = jnp.tanh(0.5 * x2) * (jnp.exp(x2) + 1.0)
    u = jnp.sqrt(jnp.maximum(-expm1, 0.0)) * (i * xc)

    d = 1
    while d < lt:
        u = u + a * _shift_rows(u, d, 0.0)
        a = a * _shift_rows(a, d, 1.0)
        d *= 2
    h = u + a * h_ref[...]
    h_ref[...] = h[lt - 1:lt]
    o_ref[...] = h * _gelu(gt_ref[...])


def _lru(xr, gate, conv_w, conv_b, wab_bf16, bab, sp):
    b, s, w = xr.shape
    lt = min(SCAN_TILE, s)
    tok = pl.BlockSpec((None, lt, w), lambda i, j: (i, j, 0))
    full = lambda a: pl.BlockSpec(a.shape, lambda i, j: (0,) * a.ndim)
    return pl.pallas_call(
        _lru_kernel,
        grid=(b, s // lt),
        in_specs=[tok, tok, full(conv_w), full(conv_b), full(wab_bf16), full(bab), full(sp)],
        out_specs=tok,
        out_shape=jax.ShapeDtypeStruct((b, s, w), F32),
        scratch_shapes=[pltpu.VMEM((8 + lt, w), F32), pltpu.VMEM((1, w), F32)],
        compiler_params=_params("arbitrary", "arbitrary"),
        name="rglru",
    )(xr, gate, conv_w, conv_b, wab_bf16, bab, sp)


def _s5_kernel(u_ref, wb_ref, apow_ref, wc_ref, d_ref, gw_ref, gb_ref, o_ref, xs_ref):
    lt = u_ref.shape[0]
    ns = S5_NS

    @pl.when(pl.program_id(1) == 0)
    def _():
        xs_ref[...] = jnp.zeros_like(xs_ref)

    u = u_ref[...]
    bu = _dot(u.astype(BF16), wb_ref[...])
    re = bu[:, :ns]
    im = bu[:, ns:]
    d = 1
    while d < lt:
        ar = apow_ref[0, d - 1:d, :]
        ai = apow_ref[1, d - 1:d, :]
        rs = _shift_rows(re, d, 0.0)
        isf = _shift_rows(im, d, 0.0)
        re, im = re + ar * rs - ai * isf, im + ar * isf + ai * rs
        d *= 2
    xr0 = xs_ref[0:1, :]
    xi0 = xs_ref[1:2, :]
    acr = apow_ref[0]
    aci = apow_ref[1]
    re, im = re + acr * xr0 - aci * xi0, im + acr * xi0 + aci * xr0
    xs_ref[0:1, :] = re[lt - 1:lt]
    xs_ref[1:2, :] = im[lt - 1:lt]
    x = jnp.concatenate([re, im], axis=1).astype(BF16)
    y = _gelu(_dot(x, wc_ref[...]) + d_ref[...] * u)
    z = _sigmoid(_dot(y.astype(BF16), gw_ref[...]) + gb_ref[...])
    o_ref[...] = y * z


def _s5(u, wb_bf16, apow, wc_bf16, d_skip, glu_w_bf16, glu_b):
    b, s, w = u.shape
    lt = apow.shape[1]
    tok = pl.BlockSpec((None, lt, w), lambda i, j: (i, j, 0))
    full = lambda a: pl.BlockSpec(a.shape, lambda i, j: (0,) * a.ndim)
    return pl.pallas_call(
        _s5_kernel,
        grid=(b, s // lt),
        in_specs=[tok, full(wb_bf16), full(apow), full(wc_bf16), full(d_skip), full(glu_w_bf16), full(glu_b)],
        out_specs=tok,
        out_shape=jax.ShapeDtypeStruct((b, s, w), F32),
        scratch_shapes=[pltpu.VMEM((2, S5_NS), F32)],
        compiler_params=_params("arbitrary", "arbitrary"),
        name="s5",
    )(u, wb_bf16, apow, wc_bf16, d_skip, glu_w_bf16, glu_b)


def _s5_tables(lam_re, lam_im, log_dt, b_re, b_im, c_re, c_im, lt):
    g, p, h = b_re.shape
    dt = jnp.exp(log_dt)[:, None]
    mag = jnp.exp(lam_re * dt)
    ab_re, ab_im = mag * jnp.cos(lam_im * dt), mag * jnp.sin(lam_im * dt)
    den = lam_re * lam_re + lam_im * lam_im
    nr, ni = ab_re - 1.0, ab_im
    co_re = (nr * lam_re + ni * lam_im) / den
    co_im = (ni * lam_re - nr * lam_im) / den
    bb_re = co_re[..., None] * b_re - co_im[..., None] * b_im
    bb_im = co_re[..., None] * b_im + co_im[..., None] * b_re
    eye = jnp.eye(g, dtype=F32)
    wb = jnp.concatenate([jnp.einsum('gph,gk->ghkp', bb_re, eye).reshape(g * h, g * p),
                          jnp.einsum('gph,gk->ghkp', bb_im, eye).reshape(g * h, g * p)], axis=1)
    wc = jnp.concatenate([jnp.einsum('ghp,gk->kpgh', c_re, eye).reshape(g * p, g * h),
                          -jnp.einsum('ghp,gk->kpgh', c_im, eye).reshape(g * p, g * h)], axis=0)

    def cmul(x, y):
        return (x[0] * y[0] - x[1] * y[1], x[0] * y[1] + x[1] * y[0])

    ar = jnp.broadcast_to(ab_re.reshape(1, g * p), (lt, g * p))
    ai = jnp.broadcast_to(ab_im.reshape(1, g * p), (lt, g * p))
    pr, pi = lax.associative_scan(cmul, (ar, ai), axis=0)
    return wb.astype(BF16), jnp.stack([pr, pi]), wc.astype(BF16)


def _outproj_kernel(oa_ref, ol_ref, os_ref, x_ref, mod_ref, mg_ref, w_ref, g2_ref, *rest, moe):
    if moe:
        rw_ref, x_out, h_out, ti_out, tw_out = rest
    else:
        x_out, h_out = rest
    mod = mod_ref[...]
    mg = mg_ref[...]
    a, l = ATT_WIDTH, LRU_WIDTH
    w = w_ref[...]
    o = (_dot(_rms(oa_ref[...], mg[:, :a]).astype(BF16), w[:a])
         + _dot(_rms(ol_ref[...], mg[:, a:a + l]).astype(BF16), w[a:a + l])
         + _dot(_rms(os_ref[...], mg[:, a + l:]).astype(BF16), w[a + l:]))
    x = x_ref[...] + mod[2:3] * o
    x_out[...] = x
    h = _rms(x, g2_ref[...]) * (1.0 + mod[4:5]) + mod[3:4]
    h_out[...] = h.astype(h_out.dtype)
    if moe:
        hb = h.astype(BF16)
        h_lo = (h - hb.astype(F32)).astype(BF16)
        rw = rw_ref[...]
        rw_hi = rw.astype(BF16)
        rw_lo = (rw - rw_hi.astype(F32)).astype(BF16)
        logits = _dot(hb, rw_hi) + _dot(hb, rw_lo) + _dot(h_lo, rw_hi)
        col = lax.broadcasted_iota(jnp.int32, logits.shape, 1)
        m1 = jnp.max(logits, axis=1, keepdims=True)
        i1 = jnp.min(jnp.where(logits == m1, col, N_EXPERTS), axis=1, keepdims=True)
        rest_l = jnp.where(col == i1, -jnp.inf, logits)
        m2 = jnp.max(rest_l, axis=1, keepdims=True)
        i2 = jnp.min(jnp.where(rest_l == m2, col, N_EXPERTS), axis=1, keepdims=True)
        e2 = jnp.exp(m2 - m1)
        den = 1.0 + e2
        ti_out[...] = jnp.concatenate([i1, i2], axis=1)
        tw_out[...] = jnp.concatenate([1.0 / den, e2 / den], axis=1)


def _outproj(o_att, o_lru, o_s5, x, mod, mix_gain, w_out_bf16, g2, router_w=None):
    b, s, d = x.shape
    tt = min(TOK_TILE, s)
    moe = router_w is not None
    tok = lambda w: pl.BlockSpec((None, tt, w), lambda i, j: (i, j, 0))
    full = lambda r, c: pl.BlockSpec((r, c), lambda i, j: (0, 0))
    in_specs = [tok(ATT_WIDTH), tok(LRU_WIDTH), tok(S5_WIDTH), tok(d),
                pl.BlockSpec((None, 6, d), lambda i, j: (i, 0, 0)), full(1, d), full(d, d), full(1, d)]
    args = [o_att, o_lru, o_s5, x, mod, mix_gain.reshape(1, d), w_out_bf16, g2.reshape(1, d)]
    out_specs = [tok(d), tok(d)]
    out_shape = [jax.ShapeDtypeStruct((b, s, d), F32),
                 jax.ShapeDtypeStruct((b, s, d), F32 if moe else BF16)]
    if moe:
        in_specs.append(full(d, N_EXPERTS))
        args.append(router_w)
        out_specs += [tok(2), tok(2)]
        out_shape += [jax.ShapeDtypeStruct((b, s, 2), jnp.int32), jax.ShapeDtypeStruct((b, s, 2), F32)]
    return pl.pallas_call(
        functools.partial(_outproj_kernel, moe=moe),
        grid=(b, s // tt),
        in_specs=in_specs, out_specs=out_specs, out_shape=out_shape,
        compiler_params=_params("arbitrary", "arbitrary"),
        name="outproj_moe" if moe else "outproj",
    )(*args)


def _ffn_kernel(h_ref, x_ref, mod_ref, wg_ref, wu_ref, wd_ref, o_ref, acc_ref):
    f = pl.program_id(2)
    h = h_ref[...]
    g = _dot(h, wg_ref[...])
    u = _dot(h, wu_ref[...])
    part = _dot(((g * _sigmoid(g)) * u).astype(BF16), wd_ref[...])

    @pl.when(f == 0)
    def _():
        acc_ref[...] = part

    @pl.when(f > 0)
    def _():
        acc_ref[...] += part

    @pl.when(f == pl.num_programs(2) - 1)
    def _():
        o_ref[...] = x_ref[...] + mod_ref[...][5:6] * acc_ref[...]


def _ffn(h, x, mod, wg, wu, wd):
    b, s, d = x.shape
    tt = min(TOK_TILE, s)
    dff = wg.shape[1]
    n_f = 2
    fc = dff // n_f
    tok = lambda w: pl.BlockSpec((None, tt, w), lambda i, j, f: (i, j, 0))
    return pl.pallas_call(
        _ffn_kernel,
        grid=(b, s // tt, n_f),
        in_specs=[tok(d), tok(d), pl.BlockSpec((None, 6, d), lambda i, j, f: (i, 0, 0)),
                  pl.BlockSpec((d, fc), lambda i, j, f: (0, f)),
                  pl.BlockSpec((d, fc), lambda i, j, f: (0, f)),
                  pl.BlockSpec((fc, d), lambda i, j, f: (f, 0))],
        out_specs=tok(d),
        out_shape=jax.ShapeDtypeStruct((b, s, d), F32),
        scratch_shapes=[pltpu.VMEM((tt, d), F32)],
        compiler_params=_params("arbitrary", "arbitrary", "arbitrary"),
        name="ffn_dense",
    )(h, x, mod, wg, wu, wd)


def _final_kernel(x_ref, g_ref, o_ref):
    o_ref[...] = _rms(x_ref[...], g_ref[...])


def _final_norm(x, g):
    b, s, d = x.shape
    tt = min(TOK_TILE, s)
    tok = pl.BlockSpec((None, tt, d), lambda i, j: (i, j, 0))
    return pl.pallas_call(
        _final_kernel, grid=(b, s // tt),
        in_specs=[tok, pl.BlockSpec((1, d), lambda i, j: (0, 0))],
        out_specs=tok, out_shape=jax.ShapeDtypeStruct((b, s, d), F32),
        compiler_params=_params("arbitrary", "arbitrary"),
        name="final_norm",
    )(x, g.reshape(1, d))


def _rope_tables(positions):
    half = HEAD_DIM // 2
    inv = 10000.0 ** (-jnp.arange(0, HEAD_DIM, 2, dtype=F32) / HEAD_DIM)
    ang = positions.astype(F32)[..., None] * inv
    cos, sin = jnp.cos(ang), jnp.sin(ang)
    reps = LANES // HEAD_DIM
    cos_t = jnp.tile(jnp.concatenate([cos, cos], axis=-1), (1, 1, reps))
    sin_t = jnp.tile(jnp.concatenate([-sin, sin], axis=-1), (1, 1, reps))
    return cos_t, sin_t


def _block_diag(w):
    h, i, j = w.shape
    eye = jnp.eye(h, dtype=w.dtype)
    return jnp.einsum('hij,hk->hikj', w, eye).reshape(h * i, h * j)


DMA_UNROLL = 8


def _moe_plan(ti):
    t = ti.shape[0]
    tm = MOE_TILE
    n_tiles = (2 * t) // tm + N_EXPERTS
    onehot = jnp.any(ti[:, :, None] == jnp.arange(N_EXPERTS, dtype=jnp.int32), axis=1)
    pos = jnp.cumsum(onehot.astype(jnp.int32), axis=0) - 1
    counts = pos[-1] + 1
    padded = ((counts + tm - 1) // tm) * tm
    ends = jnp.cumsum(padded)
    starts = ends - padded
    dest = jnp.take_along_axis(starts[None, :] + pos, ti, axis=1)
    tile_start = jnp.arange(n_tiles, dtype=jnp.int32) * tm
    tile_expert = jnp.minimum(jnp.sum((tile_start[:, None] >= ends[None, :]).astype(jnp.int32), axis=1),
                              N_EXPERTS - 1)
    n_active = (ends[-1] // tm).astype(jnp.int32).reshape(1)
    return dest.astype(jnp.int32), tile_expert, n_active


def _dispatch_kernel(dest_ref, h_ref, xs_in, xs_hbm, sem):
    del xs_in
    tt = h_ref.shape[0]

    def start(g, c):
        for k in range(DMA_UNROLL):
            r = g * DMA_UNROLL + k
            for slot in range(2):
                t = dest_ref[0, 0, 2 * r + slot]
                pltpu.make_async_copy(h_ref.at[pl.ds(r, 1), :], xs_hbm.at[pl.ds(t, 1), :], sem).start()
        return c

    lax.fori_loop(0, tt // DMA_UNROLL, start, 0)
    for slot in range(2):
        pltpu.make_async_copy(h_ref, xs_hbm.at[pl.ds(0, tt), :], sem).wait()


def _moe_dispatch(h2d, dest, n_pad):
    t, d = h2d.shape
    tt = min(TOK_TILE, t)
    return pl.pallas_call(
        _dispatch_kernel,
        grid=(t // tt,),
        in_specs=[pl.BlockSpec((1, 1, 2 * tt), lambda i: (i, 0, 0), memory_space=pltpu.SMEM),
                  pl.BlockSpec((tt, d), lambda i: (i, 0)),
                  pl.BlockSpec(memory_space=pl.ANY)],
        out_specs=pl.BlockSpec(memory_space=pl.ANY),
        out_shape=jax.ShapeDtypeStruct((n_pad, d), F32),
        scratch_shapes=[pltpu.SemaphoreType.DMA(())],
        input_output_aliases={2: 0},
        compiler_params=_params("arbitrary"),
        name="moe_dispatch",
    )(dest.reshape(t // tt, 1, 2 * tt), h2d, jnp.zeros((n_pad, d), F32))


def _experts_kernel(te_ref, na_ref, x_ref, wg_ref, wu_ref, wd_ref, y_ref, acc_ref):
    i = pl.program_id(0)
    f = pl.program_id(1)
    last = pl.num_programs(1) - 1
    active = i < na_ref[0]

    @pl.when(active)
    def _():
        x = x_ref[...].astype(BF16)
        g = _dot(x, wg_ref[...])
        u = _dot(x, wu_ref[...])
        part = _dot(((g * _sigmoid(g)) * u).astype(BF16), wd_ref[...])

        @pl.when(f == 0)
        def _():
            acc_ref[...] = part

        @pl.when(jnp.logical_and(f > 0, f < last))
        def _():
            acc_ref[...] += part

        @pl.when(f == last)
        def _():
            y_ref[...] = acc_ref[...] + part

    @pl.when(jnp.logical_and(jnp.logical_not(active), f == last))
    def _():
        y_ref[...] = jnp.zeros_like(y_ref)


def _moe_experts(xs, tile_expert, n_active, wg, wu, wd):
    n_pad, d = xs.shape
    tm = MOE_TILE
    dff = wg.shape[2]
    n_f = 2
    fc = dff // n_f
    grid_spec = pltpu.PrefetchScalarGridSpec(
        num_scalar_prefetch=2,
        grid=(n_pad // tm, n_f),
        in_specs=[pl.BlockSpec((tm, d), lambda i, f, te, na: (i, 0)),
                  pl.BlockSpec((None, d, fc), lambda i, f, te, na: (te[i], 0, f)),
                  pl.BlockSpec((None, d, fc), lambda i, f, te, na: (te[i], 0, f)),
                  pl.BlockSpec((None, fc, d), lambda i, f, te, na: (te[i], f, 0))],
        out_specs=pl.BlockSpec((tm, d), lambda i, f, te, na: (i, 0)),
        scratch_shapes=[pltpu.VMEM((tm, d), F32)])
    return pl.pallas_call(
        _experts_kernel,
        grid_spec=grid_spec,
        out_shape=jax.ShapeDtypeStruct((n_pad, d), F32),
        compiler_params=_params("arbitrary", "arbitrary"),
        name="moe_experts",
    )(tile_expert, n_active, xs, wg, wu, wd)


def _combine_kernel(dcur_ref, dnext_ref, ys_hbm, x_ref, tw_ref, mod_ref, g_ref, o_ref, buf, sem, *, final):
    i = pl.program_id(0)
    n = pl.num_programs(0)
    tt = x_ref.shape[0]

    def gather(d_ref, slot):
        def start(g, c):
            for k in range(DMA_UNROLL):
                r = g * DMA_UNROLL + k
                for choice in range(2):
                    t = d_ref[0, 0, 2 * r + choice]
                    pltpu.make_async_copy(ys_hbm.at[pl.ds(t, 1), :], buf.at[slot, choice, pl.ds(r, 1), :],
                                          sem.at[slot]).start()
            return c
        lax.fori_loop(0, tt // DMA_UNROLL, start, 0)

    @pl.when(i == 0)
    def _():
        gather(dcur_ref, 0)

    slot = i % 2

    @pl.when(i + 1 < n)
    def _():
        gather(dnext_ref, 1 - slot)

    for choice in range(2):
        pltpu.make_async_copy(ys_hbm.at[pl.ds(0, tt), :], buf.at[slot, choice], sem.at[slot]).wait()
    tw = tw_ref[...]
    y = tw[:, 0:1] * buf[slot, 0] + tw[:, 1:2] * buf[slot, 1]
    x = x_ref[...] + mod_ref[...][5:6] * y
    o_ref[...] = _rms(x, g_ref[...]) if final else x


def _moe_combine(x, ys, dest, tw, mod, final_g):
    b, s, d = x.shape
    t = b * s
    tt = min(256, s)
    n = t // tt
    per_b = s // tt
    dest3 = dest.reshape(n, 1, 2 * tt)
    return pl.pallas_call(
        functools.partial(_combine_kernel, final=final_g is not None),
        grid=(n,),
        in_specs=[pl.BlockSpec((1, 1, 2 * tt), lambda i: (i, 0, 0), memory_space=pltpu.SMEM),
                  pl.BlockSpec((1, 1, 2 * tt), lambda i: (jnp.minimum(i + 1, n - 1), 0, 0), memory_space=pltpu.SMEM),
                  pl.BlockSpec(memory_space=pl.ANY),
                  pl.BlockSpec((tt, d), lambda i: (i, 0)),
                  pl.BlockSpec((tt, 2), lambda i: (i, 0)),
                  pl.BlockSpec((None, 6, d), lambda i: (i // per_b, 0, 0)),
                  pl.BlockSpec((1, d), lambda i: (0, 0))],
        out_specs=pl.BlockSpec((tt, d), lambda i: (i, 0)),
        out_shape=jax.ShapeDtypeStruct((t, d), F32),
        scratch_shapes=[pltpu.VMEM((2, 2, tt, d), F32), pltpu.SemaphoreType.DMA((2,))],
        compiler_params=_params("arbitrary"),
        name="moe_combine",
    )(dest3, dest3, ys, x.reshape(t, d), tw, mod,
      (jnp.ones((d,), F32) if final_g is None else final_g).reshape(1, d)).reshape(b, s, d)


def kernel(x, c, positions, w_in, lru_conv_w, lru_conv_b, lru_w_a, lru_b_a, lru_w_x, lru_b_x, lru_lambda,
           s5_lambda_re, s5_lambda_im, s5_log_dt, s5_b_re, s5_b_im, s5_c_re, s5_c_im, s5_d, s5_glu_w,
           s5_glu_b, mix_gain, w_out, norm1_g, norm2_g, ada_w, ada_b, ffn_w_gate, ffn_w_up, ffn_w_down,
           router_w, moe_w_gate, moe_w_up, moe_w_down, final_g):
    b, s, d = x.shape
    depth = w_in.shape[0]
    cos_t, sin_t = _rope_tables(positions)
    mods = _adaln_mod(c, ada_w, ada_b).reshape(depth, b, 6, d)
    lt = min(SCAN_TILE, s)
    for l in range(depth):
        mod = mods[l]
        q, kaug, vaug, xr, gate, u = _inproj(x, mod, norm1_g[l], w_in[l].astype(BF16), cos_t, sin_t)
        o_att = _attention(q, kaug, vaug)
        wab = jnp.concatenate([_block_diag(lru_w_a[l]), _block_diag(lru_w_x[l])], axis=1).astype(BF16)
        bab = jnp.concatenate([lru_b_a[l], lru_b_x[l]]).reshape(1, -1)
        sp = jax.nn.softplus(-lru_lambda[l]).reshape(1, -1)
        o_lru = _lru(xr, gate, lru_conv_w[l], lru_conv_b[l].reshape(1, -1), wab, bab, sp)
        wb, apow, wc = _s5_tables(s5_lambda_re[l], s5_lambda_im[l], s5_log_dt[l], s5_b_re[l], s5_b_im[l],
                                  s5_c_re[l], s5_c_im[l], lt)
        o_s5 = _s5(u, wb, apow, wc, s5_d[l].reshape(1, -1), s5_glu_w[l].astype(BF16), s5_glu_b[l].reshape(1, -1))
        if l % 2 == 0:
            x, h = _outproj(o_att, o_lru, o_s5, x, mod, mix_gain[l], w_out[l].astype(BF16), norm2_g[l])
            e = l // 2
            x = _ffn(h, x, mod, ffn_w_gate[e].astype(BF16), ffn_w_up[e].astype(BF16), ffn_w_down[e].astype(BF16))
        else:
            x, h, ti, tw = _outproj(o_att, o_lru, o_s5, x, mod, mix_gain[l], w_out[l].astype(BF16), norm2_g[l],
                                    router_w[l // 2])
            e = l // 2
            dest, tile_expert, n_active = _moe_plan(ti.reshape(b * s, 2))
            xs = _moe_dispatch(h.reshape(b * s, d), dest, tile_expert.shape[0] * MOE_TILE)
            ys = _moe_experts(xs, tile_expert, n_active, moe_w_gate[e].astype(BF16), moe_w_up[e].astype(BF16),
                              moe_w_down[e].astype(BF16))
            x = _moe_combine(x, ys, dest, tw.reshape(b * s, 2), mod, final_g if l == depth - 1 else None)
    return x if depth % 2 == 0 else _final_norm(x, final_g)
```

```python
import functools
import math

import jax
import jax.numpy as jnp
from jax import lax
from jax.experimental import pallas as pl
from jax.experimental.pallas import tpu as pltpu

F32 = jnp.float32
BF16 = jnp.bfloat16

HEAD_DIM = 64
ATT_WIDTH = 384
LRU_WIDTH = 384
LRU_BLOCKS = 6
S5_WIDTH = 256
S5_GROUPS = 16
S5_GROUP_CH = 16
S5_STATE = 64
S5_NS = S5_GROUPS * S5_STATE
LRU_C = 8.0
MOBA_BLOCK = 256
MOBA_TOPK = 3
N_EXPERTS = 8
EPS = 1e-6
NEG = -1e30
LOG2E = math.log2(math.e)
LANES = 128
SUBLANES = 8
VMEM_LIMIT = 48 * 1024 * 1024
ATTN_VMEM_LIMIT = 56 * 1024 * 1024

TOK_TILE = 512
SCAN_TILE = 256
MOE_TILE = 512


def _params(*sem):
    return pltpu.CompilerParams(dimension_semantics=sem, vmem_limit_bytes=VMEM_LIMIT)


def _dot(a, b):
    return jnp.dot(a, b, preferred_element_type=F32)


def _dot_t(a, b):
    return lax.dot_general(a, b, (((1,), (1,)), ((), ())), preferred_element_type=F32)


def _sigmoid(x):
    return 1.0 / (1.0 + jnp.exp(-x))


def _gelu(x):
    c = math.sqrt(2.0 / math.pi)
    return 0.5 * x * (1.0 + jnp.tanh(c * (x + 0.044715 * (x * x * x))))


def _rms(x, g):
    ms = jnp.mean(x * x, axis=-1, keepdims=True)
    return (x * lax.rsqrt(ms + EPS)) * g


def _mod_kernel(c_ref, w_ref, b_ref, o_ref):
    c = c_ref[...]
    cond = (c * _sigmoid(c)).astype(BF16)
    o_ref[...] = _dot(cond, w_ref[...].astype(BF16)) + b_ref[...]


def _adaln_mod(c, ada_w, ada_b):
    n_layers, d, n = ada_w.shape
    b = c.shape[0]
    nc = n // 4
    return pl.pallas_call(
        _mod_kernel,
        grid=(n_layers, n // nc),
        in_specs=[pl.BlockSpec((b, d), lambda l, j: (0, 0)),
                  pl.BlockSpec((None, d, nc), lambda l, j: (l, 0, j)),
                  pl.BlockSpec((None, 1, nc), lambda l, j: (l, 0, j))],
        out_specs=pl.BlockSpec((None, b, nc), lambda l, j: (l, 0, j)),
        out_shape=jax.ShapeDtypeStruct((n_layers, b, n), F32),
        compiler_params=_params("arbitrary", "arbitrary"),
        name="adaln_mod",
    )(c, ada_w, ada_b.reshape(n_layers, 1, n))


def _inproj_kernel(x_ref, mod_ref, g_ref, w_ref, cos_ref, sin_ref,
                   q_ref, k_ref, v_ref, xr_ref, gt_ref, u_ref):
    mod = mod_ref[...]
    h = _rms(x_ref[...], g_ref[...]) * (1.0 + mod[1:2]) + mod[0:1]
    proj = _dot(h.astype(BF16), w_ref[...])
    cos = cos_ref[...]
    sin = sin_ref[...]
    lane = lax.broadcasted_iota(jnp.int32, cos.shape, 1)
    first_half = (lane % HEAD_DIM) < (HEAD_DIM // 2)

    def rope(t):
        outs = []
        for i in range(ATT_WIDTH // LANES):
            xi = t[:, LANES * i:LANES * (i + 1)]
            partner = jnp.where(first_half,
                                pltpu.roll(xi, LANES - HEAD_DIM // 2, 1),
                                pltpu.roll(xi, HEAD_DIM // 2, 1))
            outs.append(xi * cos + partner * sin)
        return jnp.concatenate(outs, axis=1)

    a = ATT_WIDTH
    q_ref[...] = (rope(proj[:, 0:a]) * (HEAD_DIM ** -0.5 * LOG2E)).astype(BF16)
    k = rope(proj[:, a:2 * a]).astype(BF16)
    v = proj[:, 2 * a:3 * a].astype(BF16)
    tt = k.shape[0]
    blk_id = (pl.program_id(1) * tt + lax.broadcasted_iota(jnp.int32, (tt, LANES), 0)) // MOBA_BLOCK
    onehot = jnp.where(blk_id == lane, 1.0, 0.0).astype(BF16)
    ones = jnp.ones((tt, LANES), BF16)
    for i in range(a // LANES):
        k_ref[:, 2 * LANES * i:2 * LANES * i + LANES] = k[:, LANES * i:LANES * (i + 1)]
        k_ref[:, 2 * LANES * i + LANES:2 * LANES * (i + 1)] = onehot
        v_ref[:, 2 * LANES * i:2 * LANES * i + LANES] = v[:, LANES * i:LANES * (i + 1)]
        v_ref[:, 2 * LANES * i + LANES:2 * LANES * (i + 1)] = ones
    xr_ref[...] = proj[:, 3 * a:3 * a + LRU_WIDTH]
    gt_ref[...] = proj[:, 3 * a + LRU_WIDTH:3 * a + 2 * LRU_WIDTH]
    u_ref[...] = proj[:, 3 * a + 2 * LRU_WIDTH:]


def _inproj(x, mod, g, w_in_bf16, cos_t, sin_t):
    b, s, d = x.shape
    tt = min(TOK_TILE, s)
    n_in = w_in_bf16.shape[1]
    tok = lambda w: pl.BlockSpec((None, tt, w), lambda i, j: (i, j, 0))
    full = lambda r, c: pl.BlockSpec((r, c), lambda i, j: (0, 0))
    outs = [jax.ShapeDtypeStruct((b, s, ATT_WIDTH), BF16)] + [jax.ShapeDtypeStruct((b, s, 2 * ATT_WIDTH), BF16)] * 2 + [
        jax.ShapeDtypeStruct((b, s, LRU_WIDTH), F32),
        jax.ShapeDtypeStruct((b, s, LRU_WIDTH), F32),
        jax.ShapeDtypeStruct((b, s, S5_WIDTH), F32)]
    return pl.pallas_call(
        _inproj_kernel,
        grid=(b, s // tt),
        in_specs=[tok(d), pl.BlockSpec((None, 6, d), lambda i, j: (i, 0, 0)), full(1, d),
                  full(d, n_in), tok(LANES), tok(LANES)],
        out_specs=[tok(ATT_WIDTH), tok(2 * ATT_WIDTH), tok(2 * ATT_WIDTH),
                   tok(LRU_WIDTH), tok(LRU_WIDTH), tok(S5_WIDTH)],
        out_shape=outs,
        compiler_params=_params("arbitrary", "arbitrary"),
        name="inproj",
    )(x, mod, g.reshape(1, d), w_in_bf16, cos_t, sin_t)


def _attn_kernel(q_ref, k_ref, v_ref, o_ref, kmean_ref, s_ref, mrun_ref, acc_ref, *, n_blk, n_pair):
    j = pl.program_id(1)
    blk = MOBA_BLOCK
    rows = 2 * blk
    kw = 2 * LANES

    @pl.when(j == 0)
    def _():
        for n in range(n_blk):
            for p in range(n_pair):
                kb = k_ref[n * blk:(n + 1) * blk, kw * p:kw * p + LANES]
                kmean_ref[n:n + 1, LANES * p:LANES * (p + 1)] = jnp.mean(kb.astype(F32), axis=0, keepdims=True)

    lane = lax.broadcasted_iota(jnp.int32, (blk, LANES), 1)
    bid = lax.broadcasted_iota(jnp.int32, (n_blk, rows), 0)
    valid = bid < j
    place = jnp.where(lax.broadcasted_iota(jnp.int32, (n_blk, LANES), 0)
                      == lax.broadcasted_iota(jnp.int32, (n_blk, LANES), 1), 1.0, 0.0).astype(BF16)
    own = pl.multiple_of(j * blk, blk)
    qpos = lax.broadcasted_iota(jnp.int32, (rows, blk), 0) % blk
    kpos = lax.broadcasted_iota(jnp.int32, (rows, blk), 1)
    causal = kpos <= qpos

    q_aug = []
    s_own = []
    for p in range(n_pair):
        q = q_ref[:, LANES * p:LANES * (p + 1)]
        zero = jnp.zeros_like(q)
        q2 = jnp.concatenate([jnp.where(lane < HEAD_DIM, q, zero),
                              jnp.where(lane >= HEAD_DIM, q, zero)], axis=0)
        kmean = kmean_ref[:, LANES * p:LANES * (p + 1)]
        km_hi = kmean.astype(BF16)
        km_lo = (kmean - km_hi.astype(F32)).astype(BF16)
        gw = jnp.where(valid, _dot_t(km_hi, q2) + _dot_t(km_lo, q2), NEG)
        sel = jnp.zeros(gw.shape, dtype=jnp.bool_)
        for _ in range(MOBA_TOPK):
            mx = jnp.max(gw, axis=0, keepdims=True)
            idx = jnp.min(jnp.where(gw == mx, bid, n_blk), axis=0, keepdims=True)
            pick = bid == idx
            sel = jnp.logical_or(sel, pick)
            gw = jnp.where(pick, -jnp.inf, gw)
        bias_t = jnp.where(jnp.logical_and(sel, valid), 0.0, NEG).astype(BF16)
        bias = lax.dot_general(bias_t, place, (((0,), (0,)), ((), ())), preferred_element_type=F32)
        q_aug.append(jnp.concatenate([q2, bias.astype(BF16)], axis=1))
        so = jnp.where(causal, _dot_t(q2, k_ref[pl.ds(own, blk), kw * p:kw * p + LANES]), NEG)
        s_own.append(so)
        mrun_ref[p] = so

    n_two = (j + 1) // 2

    def scores(i, carry):
        start = pl.multiple_of(i * (2 * blk), 2 * blk)
        for p in range(n_pair):
            sn = _dot_t(q_aug[p], k_ref[pl.ds(start, 2 * blk), kw * p:kw * (p + 1)])
            s_ref[p, i] = sn
            mrun_ref[p] = jnp.maximum(mrun_ref[p], jnp.maximum(sn[:, :blk], sn[:, blk:]))
        return carry

    lax.fori_loop(0, n_two, scores, 0)

    m = []
    for p in range(n_pair):
        mp = jnp.max(mrun_ref[p], axis=1, keepdims=True)
        m.append(mp)
        acc_ref[p] = _dot(jnp.exp2(s_own[p] - mp).astype(BF16), v_ref[pl.ds(own, blk), kw * p:kw * (p + 1)])

    def values(i, carry):
        start = pl.multiple_of(i * (2 * blk), 2 * blk)
        for p in range(n_pair):
            pn = jnp.exp2(s_ref[p, i] - m[p]).astype(BF16)
            acc_ref[p] += _dot(pn, v_ref[pl.ds(start, 2 * blk), kw * p:kw * (p + 1)])
        return carry

    lax.fori_loop(0, n_two, values, 0)
    for p in range(n_pair):
        acc = acc_ref[p]
        out = acc[:, :LANES] / acc[:, LANES:]
        o_ref[:, LANES * p:LANES * (p + 1)] = jnp.where(lane < HEAD_DIM, out[:blk], out[blk:])


def _attention(q, kaug, vaug):
    b, s, w = q.shape
    n_blk = s // MOBA_BLOCK
    n_pair = w // LANES
    rows = 2 * MOBA_BLOCK
    once = pl.Buffered(1)
    return pl.pallas_call(
        functools.partial(_attn_kernel, n_blk=n_blk, n_pair=n_pair),
        grid=(b, n_blk),
        in_specs=[pl.BlockSpec((None, MOBA_BLOCK, w), lambda i, j: (i, j, 0)),
                  pl.BlockSpec((None, s, 2 * w), lambda i, j: (i, 0, 0), pipeline_mode=once),
                  pl.BlockSpec((None, s, 2 * w), lambda i, j: (i, 0, 0), pipeline_mode=once)],
        out_specs=pl.BlockSpec((None, MOBA_BLOCK, w), lambda i, j: (i, j, 0)),
        out_shape=jax.ShapeDtypeStruct((b, s, w), F32),
        scratch_shapes=[pltpu.VMEM((n_blk, w), F32),
                        pltpu.VMEM((n_pair, n_blk // 2, rows, 2 * MOBA_BLOCK), F32),
                        pltpu.VMEM((n_pair, rows, MOBA_BLOCK), F32),
                        pltpu.VMEM((n_pair, rows, 2 * LANES), F32)],
        compiler_params=pltpu.CompilerParams(dimension_semantics=("arbitrary", "arbitrary"),
                                             vmem_limit_bytes=ATTN_VMEM_LIMIT),
        name="moba_attention",
    )(q, kaug, vaug)


def _lru_kernel(xr_ref, gt_ref, cw_ref, cb_ref, wab_ref, bab_ref, sp_ref, o_ref, hist_ref, h_ref):
    lt = xr_ref.shape[0]
    w = LRU_WIDTH

    @pl.when(pl.program_id(1) == 0)
    def _():
        hist_ref[0:8, :] = jnp.zeros((8, w), F32)
        h_ref[...] = jnp.zeros_like(h_ref)

    xr = xr_ref[...]
    hist_ref[8:8 + lt, :] = xr
    cw = cw_ref[...]
    n_tap = cw.shape[0]
    xc = cb_ref[...] + cw[n_tap - 1:n_tap] * xr
    for back in range(1, n_tap):
        xc = xc + cw[n_tap - 1 - back:n_tap - back] * hist_ref[8 - back:8 - back + lt, :]
    hist_ref[0:8, :] = xr[lt - 8:lt]

    ri = _sigmoid(_dot(xc.astype(BF16), wab_ref[...]) + bab_ref[...])
    r = ri[:, :w]
    i = ri[:, w:]
    log_a = (-LRU_C) * r * sp_ref[...]
    a = jnp.exp(log_a)
    expm1 = jnp.tanh(log_a) * (a * a + 1.0)
    u = jnp.sqrt(jnp.maximum(-expm1, 0.0)) * (i * xc)

    n_grp = lt // SUBLANES
    u = u.reshape(n_grp, SUBLANES, w)
    a = a.reshape(n_grp, SUBLANES, w)
    row = lax.broadcasted_iota(jnp.int32, (SUBLANES, w), 0)
    for k in range(3):
        d = 1 << k
        inside = row >= d
        u = u + jnp.where(inside, a, 0.0) * pltpu.roll(u, d, 1)
        a = a * jnp.where(inside, pltpu.roll(a, d, 1), 1.0)
    gg = _gelu(gt_ref[...])
    carry = h_ref[...]
    for g in range(n_grp):
        rows = slice(g * SUBLANES, (g + 1) * SUBLANES)
        hg = u[g] + a[g] * carry
        o_ref[rows, :] = hg * gg[rows]
        carry = hg[SUBLANES - 1:SUBLANES]
    h_ref[...] = carry


def _lru(xr, gate, conv_w, conv_b, wab_bf16, bab, sp):
    b, s, w = xr.shape
    lt = min(SCAN_TILE, s)
    tok = pl.BlockSpec((None, lt, w), lambda i, j: (i, j, 0))
    full = lambda a: pl.BlockSpec(a.shape, lambda i, j: (0,) * a.ndim)
    return pl.pallas_call(
        _lru_kernel,
        grid=(b, s // lt),
        in_specs=[tok, tok, full(conv_w), full(conv_b), full(wab_bf16), full(bab), full(sp)],
        out_specs=tok,
        out_shape=jax.ShapeDtypeStruct((b, s, w), F32),
        scratch_shapes=[pltpu.VMEM((8 + lt, w), F32), pltpu.VMEM((1, w), F32)],
        compiler_params=_params("arbitrary", "arbitrary"),
        name="rglru",
    )(xr, gate, conv_w, conv_b, wab_bf16, bab, sp)


def _s5_kernel(u_ref, wb_ref, atab_ref, wc_ref, d_ref, gw_ref, gb_ref, o_ref, xs_ref, st_ref):
    lt = u_ref.shape[0]
    ns = S5_NS
    n_grp = lt // SUBLANES

    @pl.when(pl.program_id(1) == 0)
    def _():
        xs_ref[...] = jnp.zeros_like(xs_ref)

    u = u_ref[...]
    bu = _dot(u.astype(BF16), wb_ref[...])
    re = bu[:, :ns].reshape(n_grp, SUBLANES, ns)
    im = bu[:, ns:].reshape(n_grp, SUBLANES, ns)
    for k in range(3):
        d = 1 << k
        ar = atab_ref[0, k]
        ai = atab_ref[1, k]
        rs = pltpu.roll(re, d, 1)
        isf = pltpu.roll(im, d, 1)
        re, im = re + ar * rs - ai * isf, im + ar * isf + ai * rs
    ar = atab_ref[0, 3]
    ai = atab_ref[1, 3]
    cr = xs_ref[0:1, :]
    ci = xs_ref[1:2, :]
    for g in range(n_grp):
        rg = re[g] + ar * cr - ai * ci
        ig = im[g] + ar * ci + ai * cr
        st_ref[g * SUBLANES:(g + 1) * SUBLANES, :ns] = rg
        st_ref[g * SUBLANES:(g + 1) * SUBLANES, ns:] = ig
        cr = rg[SUBLANES - 1:SUBLANES]
        ci = ig[SUBLANES - 1:SUBLANES]
    xs_ref[0:1, :] = cr
    xs_ref[1:2, :] = ci
    y = _gelu(_dot(st_ref[...].astype(BF16), wc_ref[...]) + d_ref[...] * u)
    z = _sigmoid(_dot(y.astype(BF16), gw_ref[...]) + gb_ref[...])
    o_ref[...] = y * z


def _s5(u, wb_bf16, atab, wc_bf16, d_skip, glu_w_bf16, glu_b):
    b, s, w = u.shape
    lt = min(SCAN_TILE, s)
    tok = pl.BlockSpec((None, lt, w), lambda i, j: (i, j, 0))
    full = lambda a: pl.BlockSpec(a.shape, lambda i, j: (0,) * a.ndim)
    return pl.pallas_call(
        _s5_kernel,
        grid=(b, s // lt),
        in_specs=[tok, full(wb_bf16), full(atab), full(wc_bf16), full(d_skip), full(glu_w_bf16), full(glu_b)],
        out_specs=tok,
        out_shape=jax.ShapeDtypeStruct((b, s, w), F32),
        scratch_shapes=[pltpu.VMEM((2, S5_NS), F32), pltpu.VMEM((lt, 2 * S5_NS), F32)],
        compiler_params=_params("arbitrary", "arbitrary"),
        name="s5",
    )(u, wb_bf16, atab, wc_bf16, d_skip, glu_w_bf16, glu_b)


def _s5_tables(lam_re, lam_im, log_dt, b_re, b_im, c_re, c_im):
    g, p, h = b_re.shape
    dt = jnp.exp(log_dt)[:, None]
    mag = jnp.exp(lam_re * dt)
    ab_re, ab_im = mag * jnp.cos(lam_im * dt), mag * jnp.sin(lam_im * dt)
    den = lam_re * lam_re + lam_im * lam_im
    nr, ni = ab_re - 1.0, ab_im
    co_re = (nr * lam_re + ni * lam_im) / den
    co_im = (ni * lam_re - nr * lam_im) / den
    bb_re = co_re[..., None] * b_re - co_im[..., None] * b_im
    bb_im = co_re[..., None] * b_im + co_im[..., None] * b_re
    eye = jnp.eye(g, dtype=F32)
    wb = jnp.concatenate([jnp.einsum('gph,gk->ghkp', bb_re, eye).reshape(g * h, g * p),
                          jnp.einsum('gph,gk->ghkp', bb_im, eye).reshape(g * h, g * p)], axis=1)
    wc = jnp.concatenate([jnp.einsum('ghp,gk->kpgh', c_re, eye).reshape(g * p, g * h),
                          -jnp.einsum('ghp,gk->kpgh', c_im, eye).reshape(g * p, g * h)], axis=0)

    def cmul(x, y):
        return (x[0] * y[0] - x[1] * y[1], x[0] * y[1] + x[1] * y[0])

    a1 = (ab_re.reshape(1, g * p), ab_im.reshape(1, g * p))
    a2 = cmul(a1, a1)
    a3 = cmul(a2, a1)
    a4 = cmul(a2, a2)
    pows = [a1, a2, a3, a4, cmul(a4, a1), cmul(a4, a2), cmul(a4, a3), cmul(a4, a4)]
    row = jnp.arange(SUBLANES)[:, None]
    parts = [[jnp.where(row >= d, a[c], 0.0) for d, a in ((1, a1), (2, a2), (4, a4))]
             + [jnp.concatenate([pw[c] for pw in pows], axis=0)] for c in range(2)]
    atab = jnp.stack([jnp.stack(pc) for pc in parts])
    return wb.astype(BF16), atab, wc.astype(BF16)


def _mix_residual(oa_ref, ol_ref, os_ref, x_ref, mod, mg_ref, w_ref, g2_ref):
    mg = mg_ref[...]
    a, l = ATT_WIDTH, LRU_WIDTH
    o = (_dot(_rms(oa_ref[...], mg[:, :a]).astype(BF16), w_ref[:a, :])
         + _dot(_rms(ol_ref[...], mg[:, a:a + l]).astype(BF16), w_ref[a:a + l, :])
         + _dot(_rms(os_ref[...], mg[:, a + l:]).astype(BF16), w_ref[a + l:, :]))
    x = x_ref[...] + mod[2:3] * o
    h = _rms(x, g2_ref[...]) * (1.0 + mod[4:5]) + mod[3:4]
    return x, h


def _outproj_ffn_kernel(oa_ref, ol_ref, os_ref, x_ref, mod_ref, mg_ref, w_ref, g2_ref,
                        wg_ref, wu_ref, wd_ref, fg_ref, o_ref, *, final):
    mod = mod_ref[...]
    x, h = _mix_residual(oa_ref, ol_ref, os_ref, x_ref, mod, mg_ref, w_ref, g2_ref)
    hb = h.astype(BF16)
    g = _dot(hb, wg_ref[...])
    u = _dot(hb, wu_ref[...])
    f = _dot(((g * _sigmoid(g)) * u).astype(BF16), wd_ref[...])
    x = x + mod[5:6] * f
    o_ref[...] = _rms(x, fg_ref[...]) if final else x


def _outproj_moe_kernel(oa_ref, ol_ref, os_ref, x_ref, mod_ref, mg_ref, w_ref, g2_ref, rwt_ref,
                        x_out, h_out, ti_out, tw_out):
    x, h = _mix_residual(oa_ref, ol_ref, os_ref, x_ref, mod_ref[...], mg_ref, w_ref, g2_ref)
    x_out[...] = x
    h_out[...] = h
    hb = h.astype(BF16)
    h_lo = (h - hb.astype(F32)).astype(BF16)
    rwt = rwt_ref[...]
    rw_hi = rwt.astype(BF16)
    rw_lo = (rwt - rw_hi.astype(F32)).astype(BF16)
    logits = _dot_t(rw_hi, hb) + _dot_t(rw_lo, hb) + _dot_t(rw_hi, h_lo)
    eid = lax.broadcasted_iota(jnp.int32, logits.shape, 0)
    m1 = jnp.max(logits, axis=0, keepdims=True)
    i1 = jnp.min(jnp.where(logits == m1, eid, N_EXPERTS), axis=0, keepdims=True)
    rest_l = jnp.where(eid == i1, -jnp.inf, logits)
    m2 = jnp.max(rest_l, axis=0, keepdims=True)
    i2 = jnp.min(jnp.where(rest_l == m2, eid, N_EXPERTS), axis=0, keepdims=True)
    e2 = jnp.exp(m2 - m1)
    den = 1.0 + e2
    ti_out[...] = jnp.concatenate([i1, i2], axis=0)
    tw_out[...] = jnp.concatenate([1.0 / den, e2 / den], axis=0)


def _outproj_specs(x, tt):
    b, s, d = x.shape
    tok = lambda w: pl.BlockSpec((None, tt, w), lambda i, j: (i, j, 0))
    once = lambda r, c: pl.BlockSpec((r, c), lambda i, j: (0, 0), pipeline_mode=pl.Buffered(1))
    specs = [tok(ATT_WIDTH), tok(LRU_WIDTH), tok(S5_WIDTH), tok(d),
             pl.BlockSpec((None, 6, d), lambda i, j: (i, 0, 0)), once(1, d), once(d, d), once(1, d)]
    return tok, once, specs


def _outproj_ffn(o_att, o_lru, o_s5, x, mod, mix_gain, w_out_bf16, g2, wg, wu, wd, final_g):
    b, s, d = x.shape
    tt = min(TOK_TILE, s)
    dff = wg.shape[1]
    tok, once, specs = _outproj_specs(x, tt)
    fg = jnp.ones((d,), F32) if final_g is None else final_g
    return pl.pallas_call(
        functools.partial(_outproj_ffn_kernel, final=final_g is not None),
        grid=(b, s // tt),
        in_specs=specs + [once(d, dff), once(d, dff), once(dff, d), once(1, d)],
        out_specs=tok(d),
        out_shape=jax.ShapeDtypeStruct((b, s, d), F32),
        compiler_params=pltpu.CompilerParams(dimension_semantics=("arbitrary", "arbitrary"),
                                             vmem_limit_bytes=ATTN_VMEM_LIMIT),
        name="outproj_ffn",
    )(o_att, o_lru, o_s5, x, mod, mix_gain.reshape(1, d), w_out_bf16, g2.reshape(1, d), wg, wu, wd,
      fg.reshape(1, d))


def _outproj_moe(o_att, o_lru, o_s5, x, mod, mix_gain, w_out_bf16, g2, router_w):
    b, s, d = x.shape
    tt = min(TOK_TILE, s)
    tok, once, specs = _outproj_specs(x, tt)
    top = pl.BlockSpec((None, 2, tt), lambda i, j: (i, 0, j))
    x_new, h, ti, tw = pl.pallas_call(
        _outproj_moe_kernel,
        grid=(b, s // tt),
        in_specs=specs + [once(N_EXPERTS, d)],
        out_specs=[tok(d), tok(d), top, top],
        out_shape=[jax.ShapeDtypeStruct((b, s, d), F32), jax.ShapeDtypeStruct((b, s, d), F32),
                   jax.ShapeDtypeStruct((b, 2, s), jnp.int32), jax.ShapeDtypeStruct((b, 2, s), F32)],
        compiler_params=_params("arbitrary", "arbitrary"),
        name="outproj_moe",
    )(o_att, o_lru, o_s5, x, mod, mix_gain.reshape(1, d), w_out_bf16, g2.reshape(1, d), router_w.T)
    return x_new, h, jnp.swapaxes(ti, 1, 2), jnp.swapaxes(tw, 1, 2)


def _rope_tables(positions):
    half = HEAD_DIM // 2
    inv = 10000.0 ** (-jnp.arange(0, HEAD_DIM, 2, dtype=F32) / HEAD_DIM)
    ang = positions.astype(F32)[..., None] * inv
    cos, sin = jnp.cos(ang), jnp.sin(ang)
    reps = LANES // HEAD_DIM
    cos_t = jnp.tile(jnp.concatenate([cos, cos], axis=-1), (1, 1, reps))
    sin_t = jnp.tile(jnp.concatenate([-sin, sin], axis=-1), (1, 1, reps))
    return cos_t, sin_t


def _block_diag(w):
    h, i, j = w.shape
    eye = jnp.eye(h, dtype=w.dtype)
    return jnp.einsum('hij,hk->hikj', w, eye).reshape(h * i, h * j)


def _moe_plan(ti):
    t = ti.shape[0]
    tm = MOE_TILE
    n_tiles = (2 * t) // tm + N_EXPERTS
    onehot = jnp.any(ti[:, :, None] == jnp.arange(N_EXPERTS, dtype=jnp.int32), axis=1)
    pos = jnp.cumsum(onehot.astype(jnp.int32), axis=0) - 1
    counts = pos[-1] + 1
    padded = ((counts + tm - 1) // tm) * tm
    ends = jnp.cumsum(padded)
    starts = ends - padded
    dest = jnp.take_along_axis(starts[None, :] + pos, ti, axis=1)
    tile_start = jnp.arange(n_tiles, dtype=jnp.int32) * tm
    tile_expert = jnp.minimum(jnp.sum((tile_start[:, None] >= ends[None, :]).astype(jnp.int32), axis=1),
                              N_EXPERTS - 1)
    n_active = (ends[-1] // tm).astype(jnp.int32).reshape(1)
    return dest.astype(jnp.int32), tile_expert, n_active


def _dispatch_kernel(dest_ref, h_ref, xs_in, xs_hbm, sem):
    del xs_in
    tt = h_ref.shape[0]

    for r in range(tt):
        for slot in range(2):
            t = dest_ref[0, 0, 2 * r + slot]
            pltpu.make_async_copy(h_ref.at[pl.ds(r, 1), :], xs_hbm.at[pl.ds(t, 1), :], sem).start()
    for slot in range(2):
        pltpu.make_async_copy(h_ref, xs_hbm.at[pl.ds(0, tt), :], sem).wait()


def _moe_dispatch(h2d, dest, n_pad):
    t, d = h2d.shape
    tt = min(TOK_TILE, t)
    return pl.pallas_call(
        _dispatch_kernel,
        grid=(t // tt,),
        in_specs=[pl.BlockSpec((1, 1, 2 * tt), lambda i: (i, 0, 0), memory_space=pltpu.SMEM),
                  pl.BlockSpec((tt, d), lambda i: (i, 0)),
                  pl.BlockSpec(memory_space=pl.ANY)],
        out_specs=pl.BlockSpec(memory_space=pl.ANY),
        out_shape=jax.ShapeDtypeStruct((n_pad, d), F32),
        scratch_shapes=[pltpu.SemaphoreType.DMA(())],
        input_output_aliases={2: 0},
        compiler_params=_params("arbitrary"),
        name="moe_dispatch",
    )(dest.reshape(t // tt, 1, 2 * tt), h2d, jnp.zeros((n_pad, d), F32))


def _experts_kernel(te_ref, na_ref, x_ref, wg_ref, wu_ref, wd_ref, y_ref, acc_ref):
    i = pl.program_id(0)
    f = pl.program_id(1)
    last = pl.num_programs(1) - 1
    active = i < na_ref[0]

    @pl.when(active)
    def _():
        x = x_ref[...].astype(BF16)
        g = _dot(x, wg_ref[...])
        u = _dot(x, wu_ref[...])
        part = _dot(((g * _sigmoid(g)) * u).astype(BF16), wd_ref[...])

        @pl.when(f == 0)
        def _():
            acc_ref[...] = part

        @pl.when(jnp.logical_and(f > 0, f < last))
        def _():
            acc_ref[...] += part

        @pl.when(f == last)
        def _():
            y_ref[...] = acc_ref[...] + part

    @pl.when(jnp.logical_and(jnp.logical_not(active), f == last))
    def _():
        y_ref[...] = jnp.zeros_like(y_ref)


def _moe_experts(xs, tile_expert, n_active, wg, wu, wd):
    n_pad, d = xs.shape
    tm = MOE_TILE
    dff = wg.shape[2]
    n_f = 2
    fc = dff // n_f
    grid_spec = pltpu.PrefetchScalarGridSpec(
        num_scalar_prefetch=2,
        grid=(n_pad // tm, n_f),
        in_specs=[pl.BlockSpec((tm, d), lambda i, f, te, na: (i, 0)),
                  pl.BlockSpec((None, d, fc), lambda i, f, te, na: (te[i], 0, f)),
                  pl.BlockSpec((None, d, fc), lambda i, f, te, na: (te[i], 0, f)),
                  pl.BlockSpec((None, fc, d), lambda i, f, te, na: (te[i], f, 0))],
        out_specs=pl.BlockSpec((tm, d), lambda i, f, te, na: (i, 0)),
        scratch_shapes=[pltpu.VMEM((tm, d), F32)])
    return pl.pallas_call(
        _experts_kernel,
        grid_spec=grid_spec,
        out_shape=jax.ShapeDtypeStruct((n_pad, d), F32),
        compiler_params=_params("arbitrary", "arbitrary"),
        name="moe_experts",
    )(tile_expert, n_active, xs, wg, wu, wd)


def _combine_kernel(dcur_ref, dnext_ref, ys_hbm, x_ref, tw_ref, mod_ref, g_ref, o_ref, buf, sem, *, final):
    i = pl.program_id(0)
    n = pl.num_programs(0)
    tt = x_ref.shape[0]

    def gather(d_ref, slot):
        for r in range(tt):
            for choice in range(2):
                t = d_ref[0, 0, 2 * r + choice]
                pltpu.make_async_copy(ys_hbm.at[pl.ds(t, 1), :], buf.at[slot, choice, pl.ds(r, 1), :],
                                      sem.at[slot]).start()

    @pl.when(i == 0)
    def _():
        gather(dcur_ref, 0)

    slot = i % 2
    for nxt in range(2):
        @pl.when(jnp.logical_and(i + 1 < n, slot == 1 - nxt))
        def _():
            gather(dnext_ref, nxt)

    for choice in range(2):
        pltpu.make_async_copy(ys_hbm.at[pl.ds(0, tt), :], buf.at[slot, choice], sem.at[slot]).wait()
    tw = tw_ref[...]
    y = tw[:, 0:1] * buf[slot, 0] + tw[:, 1:2] * buf[slot, 1]
    x = x_ref[...] + mod_ref[...][5:6] * y
    o_ref[...] = _rms(x, g_ref[...]) if final else x


def _moe_combine(x, ys, dest, tw, mod, final_g):
    b, s, d = x.shape
    t = b * s
    tt = min(256, s)
    n = t // tt
    per_b = s // tt
    dest3 = dest.reshape(n, 1, 2 * tt)
    return pl.pallas_call(
        functools.partial(_combine_kernel, final=final_g is not None),
        grid=(n,),
        in_specs=[pl.BlockSpec((1, 1, 2 * tt), lambda i: (i, 0, 0), memory_space=pltpu.SMEM),
                  pl.BlockSpec((1, 1, 2 * tt), lambda i: (jnp.minimum(i + 1, n - 1), 0, 0), memory_space=pltpu.SMEM),
                  pl.BlockSpec(memory_space=pl.ANY),
                  pl.BlockSpec((tt, d), lambda i: (i, 0)),
                  pl.BlockSpec((tt, 2), lambda i: (i, 0)),
                  pl.BlockSpec((None, 6, d), lambda i: (i // per_b, 0, 0)),
                  pl.BlockSpec((1, d), lambda i: (0, 0))],
        out_specs=pl.BlockSpec((tt, d), lambda i: (i, 0)),
        out_shape=jax.ShapeDtypeStruct((t, d), F32),
        scratch_shapes=[pltpu.VMEM((2, 2, tt, d), F32), pltpu.SemaphoreType.DMA((2,))],
        compiler_params=_params("arbitrary"),
        name="moe_combine",
    )(dest3, dest3, ys, x.reshape(t, d), tw, mod,
      (jnp.ones((d,), F32) if final_g is None else final_g).reshape(1, d)).reshape(b, s, d)


def kernel(x, c, positions, w_in, lru_conv_w, lru_conv_b, lru_w_a, lru_b_a, lru_w_x, lru_b_x, lru_lambda,
           s5_lambda_re, s5_lambda_im, s5_log_dt, s5_b_re, s5_b_im, s5_c_re, s5_c_im, s5_d, s5_glu_w,
           s5_glu_b, mix_gain, w_out, norm1_g, norm2_g, ada_w, ada_b, ffn_w_gate, ffn_w_up, ffn_w_down,
           router_w, moe_w_gate, moe_w_up, moe_w_down, final_g):
    b, s, d = x.shape
    depth = w_in.shape[0]
    cos_t, sin_t = _rope_tables(positions)
    mods = _adaln_mod(c, ada_w, ada_b).reshape(depth, b, 6, d)
    lt = min(SCAN_TILE, s)
    for l in range(depth):
        mod = mods[l]
        q, kaug, vaug, xr, gate, u = _inproj(x, mod, norm1_g[l], w_in[l].astype(BF16), cos_t, sin_t)
        o_att = _attention(q, kaug, vaug)
        wab = jnp.concatenate([_block_diag(lru_w_a[l]), _block_diag(lru_w_x[l])], axis=1).astype(BF16)
        bab = jnp.concatenate([lru_b_a[l], lru_b_x[l]]).reshape(1, -1)
        sp = jax.nn.softplus(-lru_lambda[l]).reshape(1, -1)
        o_lru = _lru(xr, gate, lru_conv_w[l], lru_conv_b[l].reshape(1, -1), wab, bab, sp)
        wb, apow, wc = _s5_tables(s5_lambda_re[l], s5_lambda_im[l], s5_log_dt[l], s5_b_re[l], s5_b_im[l],
                                  s5_c_re[l], s5_c_im[l])
        o_s5 = _s5(u, wb, apow, wc, s5_d[l].reshape(1, -1), s5_glu_w[l].astype(BF16), s5_glu_b[l].reshape(1, -1))
        fg = final_g if l == depth - 1 else None
        e = l // 2
        if l % 2 == 0:
            x = _outproj_ffn(o_att, o_lru, o_s5, x, mod, mix_gain[l], w_out[l].astype(BF16), norm2_g[l],
                             ffn_w_gate[e].astype(BF16), ffn_w_up[e].astype(BF16), ffn_w_down[e].astype(BF16), fg)
        else:
            x, h, ti, tw = _outproj_moe(o_att, o_lru, o_s5, x, mod, mix_gain[l], w_out[l].astype(BF16),
                                        norm2_g[l], router_w[e])
            dest, tile_expert, n_active = _moe_plan(ti.reshape(b * s, 2))
            xs = _moe_dispatch(h.reshape(b * s, d), dest, tile_expert.shape[0] * MOE_TILE)
            ys = _moe_experts(xs, tile_expert, n_active, moe_w_gate[e].astype(BF16), moe_w_up[e].astype(BF16),
                              moe_w_down[e].astype(BF16))
            x = _moe_combine(x, ys, dest, tw.reshape(b * s, 2), mod, fg)
    return x
```

```python
import functools
import math

import jax
import jax.numpy as jnp
from jax import lax
from jax.experimental import pallas as pl
from jax.experimental.pallas import tpu as pltpu

F32 = jnp.float32
BF16 = jnp.bfloat16

HEAD_DIM = 64
ATT_WIDTH = 384
LRU_WIDTH = 384
LRU_BLOCKS = 6
S5_WIDTH = 256
S5_GROUPS = 16
S5_GROUP_CH = 16
S5_STATE = 64
S5_NS = S5_GROUPS * S5_STATE
LRU_C = 8.0
MOBA_BLOCK = 256
MOBA_TOPK = 3
N_EXPERTS = 8
EPS = 1e-6
NEG = -1e30
LOG2E = math.log2(math.e)
LANES = 128
SUBLANES = 8
VMEM_LIMIT = 48 * 1024 * 1024
ATTN_VMEM_LIMIT = 56 * 1024 * 1024

TOK_TILE = 512
SCAN_TILE = 256
MOE_TILE = 512


def _params(*sem):
    return pltpu.CompilerParams(dimension_semantics=sem, vmem_limit_bytes=VMEM_LIMIT)


def _dot(a, b):
    return jnp.dot(a, b, preferred_element_type=F32)


def _dot_t(a, b):
    return lax.dot_general(a, b, (((1,), (1,)), ((), ())), preferred_element_type=F32)


def _sigmoid(x):
    return 1.0 / (1.0 + jnp.exp(-x))


def _gelu(x):
    c = math.sqrt(2.0 / math.pi)
    return 0.5 * x * (1.0 + jnp.tanh(c * (x + 0.044715 * (x * x * x))))


def _rms(x, g):
    ms = jnp.mean(x * x, axis=-1, keepdims=True)
    return (x * lax.rsqrt(ms + EPS)) * g


def _mod_kernel(c_ref, w_ref, b_ref, o_ref):
    c = c_ref[...]
    cond = (c * _sigmoid(c)).astype(BF16)
    o_ref[...] = _dot(cond, w_ref[...].astype(BF16)) + b_ref[...]


def _adaln_mod(c, ada_w, ada_b):
    n_layers, d, n = ada_w.shape
    b = c.shape[0]
    nc = n // 4
    return pl.pallas_call(
        _mod_kernel,
        grid=(n_layers, n // nc),
        in_specs=[pl.BlockSpec((b, d), lambda l, j: (0, 0)),
                  pl.BlockSpec((None, d, nc), lambda l, j: (l, 0, j)),
                  pl.BlockSpec((None, 1, nc), lambda l, j: (l, 0, j))],
        out_specs=pl.BlockSpec((None, b, nc), lambda l, j: (l, 0, j)),
        out_shape=jax.ShapeDtypeStruct((n_layers, b, n), F32),
        compiler_params=_params("arbitrary", "arbitrary"),
        name="adaln_mod",
    )(c, ada_w, ada_b.reshape(n_layers, 1, n))


def _inproj_kernel(x_ref, mod_ref, g_ref, w_ref, cos_ref, sin_ref,
                   q_ref, k_ref, v_ref, xr_ref, gt_ref, u_ref):
    mod = mod_ref[...]
    h = _rms(x_ref[...], g_ref[...]) * (1.0 + mod[1:2]) + mod[0:1]
    proj = _dot(h.astype(BF16), w_ref[...])
    cos = cos_ref[...]
    sin = sin_ref[...]
    lane = lax.broadcasted_iota(jnp.int32, cos.shape, 1)
    first_half = (lane % HEAD_DIM) < (HEAD_DIM // 2)

    def rope(t):
        outs = []
        for i in range(ATT_WIDTH // LANES):
            xi = t[:, LANES * i:LANES * (i + 1)]
            partner = jnp.where(first_half,
                                pltpu.roll(xi, LANES - HEAD_DIM // 2, 1),
                                pltpu.roll(xi, HEAD_DIM // 2, 1))
            outs.append(xi * cos + partner * sin)
        return jnp.concatenate(outs, axis=1)

    a = ATT_WIDTH
    q_ref[...] = (rope(proj[:, 0:a]) * (HEAD_DIM ** -0.5 * LOG2E)).astype(BF16)
    k = rope(proj[:, a:2 * a]).astype(BF16)
    v = proj[:, 2 * a:3 * a].astype(BF16)
    tt = k.shape[0]
    blk_id = (pl.program_id(1) * tt + lax.broadcasted_iota(jnp.int32, (tt, LANES), 0)) // MOBA_BLOCK
    onehot = jnp.where(blk_id == lane, 1.0, 0.0).astype(BF16)
    ones = jnp.ones((tt, LANES), BF16)
    for i in range(a // LANES):
        k_ref[:, 2 * LANES * i:2 * LANES * i + LANES] = k[:, LANES * i:LANES * (i + 1)]
        k_ref[:, 2 * LANES * i + LANES:2 * LANES * (i + 1)] = onehot
        v_ref[:, 2 * LANES * i:2 * LANES * i + LANES] = v[:, LANES * i:LANES * (i + 1)]
        v_ref[:, 2 * LANES * i + LANES:2 * LANES * (i + 1)] = ones
    xr_ref[...] = proj[:, 3 * a:3 * a + LRU_WIDTH]
    gt_ref[...] = proj[:, 3 * a + LRU_WIDTH:3 * a + 2 * LRU_WIDTH]
    u_ref[...] = proj[:, 3 * a + 2 * LRU_WIDTH:]


def _inproj(x, mod, g, w_in_bf16, cos_t, sin_t):
    b, s, d = x.shape
    tt = min(TOK_TILE, s)
    n_in = w_in_bf16.shape[1]
    tok = lambda w: pl.BlockSpec((None, tt, w), lambda i, j: (i, j, 0))
    full = lambda r, c: pl.BlockSpec((r, c), lambda i, j: (0, 0))
    outs = [jax.ShapeDtypeStruct((b, s, ATT_WIDTH), BF16)] + [jax.ShapeDtypeStruct((b, s, 2 * ATT_WIDTH), BF16)] * 2 + [
        jax.ShapeDtypeStruct((b, s, LRU_WIDTH), F32),
        jax.ShapeDtypeStruct((b, s, LRU_WIDTH), F32),
        jax.ShapeDtypeStruct((b, s, S5_WIDTH), F32)]
    return pl.pallas_call(
        _inproj_kernel,
        grid=(b, s // tt),
        in_specs=[tok(d), pl.BlockSpec((None, 6, d), lambda i, j: (i, 0, 0)), full(1, d),
                  full(d, n_in), tok(LANES), tok(LANES)],
        out_specs=[tok(ATT_WIDTH), tok(2 * ATT_WIDTH), tok(2 * ATT_WIDTH),
                   tok(LRU_WIDTH), tok(LRU_WIDTH), tok(S5_WIDTH)],
        out_shape=outs,
        compiler_params=_params("arbitrary", "arbitrary"),
        name="inproj",
    )(x, mod, g.reshape(1, d), w_in_bf16, cos_t, sin_t)


def _attn_kernel(q_ref, k_ref, v_ref, o_ref, kmean_ref, s_ref, mrun_ref, acc_ref, *, n_blk, n_pair):
    j = pl.program_id(1)
    blk = MOBA_BLOCK
    rows = 2 * blk
    kw = 2 * LANES

    @pl.when(j == 0)
    def _():
        for n in range(n_blk):
            for p in range(n_pair):
                kb = k_ref[n * blk:(n + 1) * blk, kw * p:kw * p + LANES]
                kmean_ref[n:n + 1, LANES * p:LANES * (p + 1)] = jnp.mean(kb.astype(F32), axis=0, keepdims=True)

    lane = lax.broadcasted_iota(jnp.int32, (blk, LANES), 1)
    bid = lax.broadcasted_iota(jnp.int32, (n_blk, rows), 0)
    valid = bid < j
    place = jnp.where(lax.broadcasted_iota(jnp.int32, (n_blk, LANES), 0)
                      == lax.broadcasted_iota(jnp.int32, (n_blk, LANES), 1), 1.0, 0.0).astype(BF16)
    own = pl.multiple_of(j * blk, blk)
    qpos = lax.broadcasted_iota(jnp.int32, (rows, blk), 0) % blk
    kpos = lax.broadcasted_iota(jnp.int32, (rows, blk), 1)
    causal = kpos <= qpos

    q_aug = []
    s_own = []
    for p in range(n_pair):
        q = q_ref[:, LANES * p:LANES * (p + 1)]
        zero = jnp.zeros_like(q)
        q2 = jnp.concatenate([jnp.where(lane < HEAD_DIM, q, zero),
                              jnp.where(lane >= HEAD_DIM, q, zero)], axis=0)
        kmean = kmean_ref[:, LANES * p:LANES * (p + 1)]
        km_hi = kmean.astype(BF16)
        km_lo = (kmean - km_hi.astype(F32)).astype(BF16)
        gw = jnp.where(valid, _dot_t(km_hi, q2) + _dot_t(km_lo, q2), NEG)
        sel = jnp.zeros(gw.shape, dtype=jnp.bool_)
        for _ in range(MOBA_TOPK):
            mx = jnp.max(gw, axis=0, keepdims=True)
            idx = jnp.min(jnp.where(gw == mx, bid, n_blk), axis=0, keepdims=True)
            pick = bid == idx
            sel = jnp.logical_or(sel, pick)
            gw = jnp.where(pick, -jnp.inf, gw)
        bias_t = jnp.where(jnp.logical_and(sel, valid), 0.0, NEG).astype(BF16)
        bias = lax.dot_general(bias_t, place, (((0,), (0,)), ((), ())), preferred_element_type=F32)
        q_aug.append(jnp.concatenate([q2, bias.astype(BF16)], axis=1))
        so = jnp.where(causal, _dot_t(q2, k_ref[pl.ds(own, blk), kw * p:kw * p + LANES]), NEG)
        s_own.append(so)
        mrun_ref[p] = so

    n_two = (j + 1) // 2

    def scores(i, carry):
        start = pl.multiple_of(i * (2 * blk), 2 * blk)
        for p in range(n_pair):
            sn = _dot_t(q_aug[p], k_ref[pl.ds(start, 2 * blk), kw * p:kw * (p + 1)])
            s_ref[p, i] = sn
            mrun_ref[p] = jnp.maximum(mrun_ref[p], jnp.maximum(sn[:, :blk], sn[:, blk:]))
        return carry

    lax.fori_loop(0, n_two, scores, 0)

    m = []
    for p in range(n_pair):
        mp = jnp.max(mrun_ref[p], axis=1, keepdims=True)
        m.append(mp)
        acc_ref[p] = _dot(jnp.exp2(s_own[p] - mp).astype(BF16), v_ref[pl.ds(own, blk), kw * p:kw * (p + 1)])

    def values(i, carry):
        start = pl.multiple_of(i * (2 * blk), 2 * blk)
        for p in range(n_pair):
            pn = jnp.exp2(s_ref[p, i] - m[p]).astype(BF16)
            acc_ref[p] += _dot(pn, v_ref[pl.ds(start, 2 * blk), kw * p:kw * (p + 1)])
        return carry

    lax.fori_loop(0, n_two, values, 0)
    for p in range(n_pair):
        acc = acc_ref[p]
        out = acc[:, :LANES] / acc[:, LANES:]
        o_ref[:, LANES * p:LANES * (p + 1)] = jnp.where(lane < HEAD_DIM, out[:blk], out[blk:])


def _attention(q, kaug, vaug):
    b, s, w = q.shape
    n_blk = s // MOBA_BLOCK
    n_pair = w // LANES
    rows = 2 * MOBA_BLOCK
    once = pl.Buffered(1)
    return pl.pallas_call(
        functools.partial(_attn_kernel, n_blk=n_blk, n_pair=n_pair),
        grid=(b, n_blk),
        in_specs=[pl.BlockSpec((None, MOBA_BLOCK, w), lambda i, j: (i, j, 0)),
                  pl.BlockSpec((None, s, 2 * w), lambda i, j: (i, 0, 0), pipeline_mode=once),
                  pl.BlockSpec((None, s, 2 * w), lambda i, j: (i, 0, 0), pipeline_mode=once)],
        out_specs=pl.BlockSpec((None, MOBA_BLOCK, w), lambda i, j: (i, j, 0)),
        out_shape=jax.ShapeDtypeStruct((b, s, w), F32),
        scratch_shapes=[pltpu.VMEM((n_blk, w), F32),
                        pltpu.VMEM((n_pair, n_blk // 2, rows, 2 * MOBA_BLOCK), F32),
                        pltpu.VMEM((n_pair, rows, MOBA_BLOCK), F32),
                        pltpu.VMEM((n_pair, rows, 2 * LANES), F32)],
        compiler_params=pltpu.CompilerParams(dimension_semantics=("arbitrary", "arbitrary"),
                                             vmem_limit_bytes=ATTN_VMEM_LIMIT),
        name="moba_attention",
    )(q, kaug, vaug)


def _lru_kernel(xr_ref, gt_ref, cw_ref, cb_ref, wab_ref, bab_ref, sp_ref, o_ref, hist_ref, h_ref):
    lt = xr_ref.shape[0]
    w = LRU_WIDTH

    @pl.when(pl.program_id(1) == 0)
    def _():
        hist_ref[0:8, :] = jnp.zeros((8, w), F32)
        h_ref[...] = jnp.zeros_like(h_ref)

    xr = xr_ref[...]
    hist_ref[8:8 + lt, :] = xr
    cw = cw_ref[...]
    n_tap = cw.shape[0]
    xc = cb_ref[...] + cw[n_tap - 1:n_tap] * xr
    for back in range(1, n_tap):
        xc = xc + cw[n_tap - 1 - back:n_tap - back] * hist_ref[8 - back:8 - back + lt, :]
    hist_ref[0:8, :] = xr[lt - 8:lt]

    ri = _sigmoid(_dot(xc.astype(BF16), wab_ref[...]) + bab_ref[...])
    r = ri[:, :w]
    i = ri[:, w:]
    log_a = (-LRU_C) * r * sp_ref[...]
    a = jnp.exp(log_a)
    expm1 = jnp.tanh(log_a) * (a * a + 1.0)
    u = jnp.sqrt(jnp.maximum(-expm1, 0.0)) * (i * xc)

    n_grp = lt // SUBLANES
    u = u.reshape(n_grp, SUBLANES, w)
    a = a.reshape(n_grp, SUBLANES, w)
    row = lax.broadcasted_iota(jnp.int32, (SUBLANES, w), 0)
    for k in range(3):
        d = 1 << k
        inside = row >= d
        u = u + jnp.where(inside, a, 0.0) * pltpu.roll(u, d, 1)
        a = a * jnp.where(inside, pltpu.roll(a, d, 1), 1.0)
    gg = _gelu(gt_ref[...])
    carry = h_ref[...]
    for g in range(n_grp):
        rows = slice(g * SUBLANES, (g + 1) * SUBLANES)
        hg = u[g] + a[g] * carry
        o_ref[rows, :] = hg * gg[rows]
        carry = hg[SUBLANES - 1:SUBLANES]
    h_ref[...] = carry


def _lru(xr, gate, conv_w, conv_b, wab_bf16, bab, sp):
    b, s, w = xr.shape
    lt = min(SCAN_TILE, s)
    tok = pl.BlockSpec((None, lt, w), lambda i, j: (i, j, 0))
    full = lambda a: pl.BlockSpec(a.shape, lambda i, j: (0,) * a.ndim)
    return pl.pallas_call(
        _lru_kernel,
        grid=(b, s // lt),
        in_specs=[tok, tok, full(conv_w), full(conv_b), full(wab_bf16), full(bab), full(sp)],
        out_specs=tok,
        out_shape=jax.ShapeDtypeStruct((b, s, w), F32),
        scratch_shapes=[pltpu.VMEM((8 + lt, w), F32), pltpu.VMEM((1, w), F32)],
        compiler_params=_params("arbitrary", "arbitrary"),
        name="rglru",
    )(xr, gate, conv_w, conv_b, wab_bf16, bab, sp)


S5_STEPS = 64


def _s5_kernel(u_ref, wb_ref, a_ref, wc_ref, d_ref, gw_ref, gb_ref, o_ref, xs_ref, bu_ref, st_ref):
    nb, lt, w = u_ref.shape
    ns = S5_NS
    rows = nb * lt

    @pl.when(pl.program_id(0) == 0)
    def _():
        xs_ref[...] = jnp.zeros_like(xs_ref)

    u = u_ref[...].reshape(rows, w)
    r = lax.broadcasted_iota(jnp.int32, (rows, rows), 0)
    c = lax.broadcasted_iota(jnp.int32, (rows, rows), 1)
    to_tm = jnp.where((r % nb) * lt + r // nb == c, 1.0, 0.0).astype(BF16)
    to_bt = jnp.where((r % lt) * nb + r // lt == c, 1.0, 0.0).astype(BF16)
    u_tm = _dot(to_tm, u.astype(BF16)).astype(BF16)
    bu_ref[...] = _dot(u_tm, wb_ref[...]).reshape(lt, nb, 2 * ns)
    ar = jnp.broadcast_to(a_ref[0:1, :], (nb, ns))
    ai = jnp.broadcast_to(a_ref[1:2, :], (nb, ns))

    def step(t, carry):
        re, im = carry
        b = bu_ref[t]
        re, im = ar * re - ai * im + b[:, :ns], ar * im + ai * re + b[:, ns:]
        st_ref[t] = jnp.concatenate([re, im], axis=1)
        return re, im

    x0 = xs_ref[...]
    carry = (x0[:, :ns], x0[:, ns:])
    for t in range(lt):
        carry = step(t, carry)
    re, im = carry
    xs_ref[...] = jnp.concatenate([re, im], axis=1)
    y_tm = _dot(st_ref[...].reshape(rows, 2 * ns).astype(BF16), wc_ref[...])
    y_hi = y_tm.astype(BF16)
    y_lo = (y_tm - y_hi.astype(F32)).astype(BF16)
    y_bt = _dot(to_bt, y_hi) + _dot(to_bt, y_lo)
    y = _gelu(y_bt + d_ref[...] * u)
    z = _sigmoid(_dot(y.astype(BF16), gw_ref[...]) + gb_ref[...])
    o_ref[...] = (y * z).reshape(nb, lt, w)


def _s5(u, wb_bf16, a1, wc_bf16, d_skip, glu_w_bf16, glu_b):
    b, s, w = u.shape
    lt = min(S5_STEPS, s)
    tok = pl.BlockSpec((b, lt, w), lambda i: (0, i, 0))
    full = lambda a: pl.BlockSpec(a.shape, lambda i: (0,) * a.ndim)
    return pl.pallas_call(
        _s5_kernel,
        grid=(s // lt,),
        in_specs=[tok, full(wb_bf16), full(a1), full(wc_bf16), full(d_skip), full(glu_w_bf16), full(glu_b)],
        out_specs=tok,
        out_shape=jax.ShapeDtypeStruct((b, s, w), F32),
        scratch_shapes=[pltpu.VMEM((b, 2 * S5_NS), F32), pltpu.VMEM((lt, b, 2 * S5_NS), F32),
                        pltpu.VMEM((lt, b, 2 * S5_NS), F32)],
        compiler_params=_params("arbitrary"),
        name="s5",
    )(u, wb_bf16, a1, wc_bf16, d_skip, glu_w_bf16, glu_b)


def _s5_tables(lam_re, lam_im, log_dt, b_re, b_im, c_re, c_im):
    g, p, h = b_re.shape
    dt = jnp.exp(log_dt)[:, None]
    mag = jnp.exp(lam_re * dt)
    ab_re, ab_im = mag * jnp.cos(lam_im * dt), mag * jnp.sin(lam_im * dt)
    den = lam_re * lam_re + lam_im * lam_im
    nr, ni = ab_re - 1.0, ab_im
    co_re = (nr * lam_re + ni * lam_im) / den
    co_im = (ni * lam_re - nr * lam_im) / den
    bb_re = co_re[..., None] * b_re - co_im[..., None] * b_im
    bb_im = co_re[..., None] * b_im + co_im[..., None] * b_re
    eye = jnp.eye(g, dtype=F32)
    wb = jnp.concatenate([jnp.einsum('gph,gk->ghkp', bb_re, eye).reshape(g * h, g * p),
                          jnp.einsum('gph,gk->ghkp', bb_im, eye).reshape(g * h, g * p)], axis=1)
    wc = jnp.concatenate([jnp.einsum('ghp,gk->kpgh', c_re, eye).reshape(g * p, g * h),
                          -jnp.einsum('ghp,gk->kpgh', c_im, eye).reshape(g * p, g * h)], axis=0)

    a1 = jnp.stack([ab_re.reshape(g * p), ab_im.reshape(g * p)])
    return wb.astype(BF16), a1, wc.astype(BF16)


def _mix_residual(oa_ref, ol_ref, os_ref, x_ref, mod, mg_ref, w_ref, g2_ref):
    mg = mg_ref[...]
    a, l = ATT_WIDTH, LRU_WIDTH
    o = (_dot(_rms(oa_ref[...], mg[:, :a]).astype(BF16), w_ref[:a, :])
         + _dot(_rms(ol_ref[...], mg[:, a:a + l]).astype(BF16), w_ref[a:a + l, :])
         + _dot(_rms(os_ref[...], mg[:, a + l:]).astype(BF16), w_ref[a + l:, :]))
    x = x_ref[...] + mod[2:3] * o
    h = _rms(x, g2_ref[...]) * (1.0 + mod[4:5]) + mod[3:4]
    return x, h


def _outproj_ffn_kernel(oa_ref, ol_ref, os_ref, x_ref, mod_ref, mg_ref, w_ref, g2_ref,
                        wg_ref, wu_ref, wd_ref, fg_ref, o_ref, *, final):
    mod = mod_ref[...]
    x, h = _mix_residual(oa_ref, ol_ref, os_ref, x_ref, mod, mg_ref, w_ref, g2_ref)
    hb = h.astype(BF16)
    g = _dot(hb, wg_ref[...])
    u = _dot(hb, wu_ref[...])
    f = _dot(((g * _sigmoid(g)) * u).astype(BF16), wd_ref[...])
    x = x + mod[5:6] * f
    o_ref[...] = _rms(x, fg_ref[...]) if final else x


def _outproj_moe_kernel(oa_ref, ol_ref, os_ref, x_ref, mod_ref, mg_ref, w_ref, g2_ref, rwt_ref,
                        x_out, h_out, ti_out, tw_out):
    x, h = _mix_residual(oa_ref, ol_ref, os_ref, x_ref, mod_ref[...], mg_ref, w_ref, g2_ref)
    x_out[...] = x
    h_out[...] = h
    hb = h.astype(BF16)
    h_lo = (h - hb.astype(F32)).astype(BF16)
    rwt = rwt_ref[...]
    rw_hi = rwt.astype(BF16)
    rw_lo = (rwt - rw_hi.astype(F32)).astype(BF16)
    logits = _dot_t(rw_hi, hb) + _dot_t(rw_lo, hb) + _dot_t(rw_hi, h_lo)
    eid = lax.broadcasted_iota(jnp.int32, logits.shape, 0)
    m1 = jnp.max(logits, axis=0, keepdims=True)
    i1 = jnp.min(jnp.where(logits == m1, eid, N_EXPERTS), axis=0, keepdims=True)
    rest_l = jnp.where(eid == i1, -jnp.inf, logits)
    m2 = jnp.max(rest_l, axis=0, keepdims=True)
    i2 = jnp.min(jnp.where(rest_l == m2, eid, N_EXPERTS), axis=0, keepdims=True)
    e2 = jnp.exp(m2 - m1)
    den = 1.0 + e2
    ti_out[...] = jnp.concatenate([i1, i2], axis=0)
    tw_out[...] = jnp.concatenate([1.0 / den, e2 / den], axis=0)


def _outproj_specs(x, tt):
    b, s, d = x.shape
    tok = lambda w: pl.BlockSpec((None, tt, w), lambda i, j: (i, j, 0))
    once = lambda r, c: pl.BlockSpec((r, c), lambda i, j: (0, 0), pipeline_mode=pl.Buffered(1))
    specs = [tok(ATT_WIDTH), tok(LRU_WIDTH), tok(S5_WIDTH), tok(d),
             pl.BlockSpec((None, 6, d), lambda i, j: (i, 0, 0)), once(1, d), once(d, d), once(1, d)]
    return tok, once, specs


def _outproj_ffn(o_att, o_lru, o_s5, x, mod, mix_gain, w_out_bf16, g2, wg, wu, wd, final_g):
    b, s, d = x.shape
    tt = min(TOK_TILE, s)
    dff = wg.shape[1]
    tok, once, specs = _outproj_specs(x, tt)
    fg = jnp.ones((d,), F32) if final_g is None else final_g
    return pl.pallas_call(
        functools.partial(_outproj_ffn_kernel, final=final_g is not None),
        grid=(b, s // tt),
        in_specs=specs + [once(d, dff), once(d, dff), once(dff, d), once(1, d)],
        out_specs=tok(d),
        out_shape=jax.ShapeDtypeStruct((b, s, d), F32),
        compiler_params=pltpu.CompilerParams(dimension_semantics=("arbitrary", "arbitrary"),
                                             vmem_limit_bytes=ATTN_VMEM_LIMIT),
        name="outproj_ffn",
    )(o_att, o_lru, o_s5, x, mod, mix_gain.reshape(1, d), w_out_bf16, g2.reshape(1, d), wg, wu, wd,
      fg.reshape(1, d))


def _outproj_moe(o_att, o_lru, o_s5, x, mod, mix_gain, w_out_bf16, g2, router_w):
    b, s, d = x.shape
    tt = min(TOK_TILE, s)
    tok, once, specs = _outproj_specs(x, tt)
    top = pl.BlockSpec((None, 2, tt), lambda i, j: (i, 0, j))
    x_new, h, ti, tw = pl.pallas_call(
        _outproj_moe_kernel,
        grid=(b, s // tt),
        in_specs=specs + [once(N_EXPERTS, d)],
        out_specs=[tok(d), tok(d), top, top],
        out_shape=[jax.ShapeDtypeStruct((b, s, d), F32), jax.ShapeDtypeStruct((b, s, d), F32),
                   jax.ShapeDtypeStruct((b, 2, s), jnp.int32), jax.ShapeDtypeStruct((b, 2, s), F32)],
        compiler_params=_params("arbitrary", "arbitrary"),
        name="outproj_moe",
    )(o_att, o_lru, o_s5, x, mod, mix_gain.reshape(1, d), w_out_bf16, g2.reshape(1, d), router_w.T)
    return x_new, h, jnp.swapaxes(ti, 1, 2), jnp.swapaxes(tw, 1, 2)


def _rope_tables(positions):
    half = HEAD_DIM // 2
    inv = 10000.0 ** (-jnp.arange(0, HEAD_DIM, 2, dtype=F32) / HEAD_DIM)
    ang = positions.astype(F32)[..., None] * inv
    cos, sin = jnp.cos(ang), jnp.sin(ang)
    reps = LANES // HEAD_DIM
    cos_t = jnp.tile(jnp.concatenate([cos, cos], axis=-1), (1, 1, reps))
    sin_t = jnp.tile(jnp.concatenate([-sin, sin], axis=-1), (1, 1, reps))
    return cos_t, sin_t


def _block_diag(w):
    h, i, j = w.shape
    eye = jnp.eye(h, dtype=w.dtype)
    return jnp.einsum('hij,hk->hikj', w, eye).reshape(h * i, h * j)


def _moe_plan(ti):
    t = ti.shape[0]
    tm = MOE_TILE
    n_tiles = (2 * t) // tm + N_EXPERTS
    onehot = jnp.any(ti[:, :, None] == jnp.arange(N_EXPERTS, dtype=jnp.int32), axis=1)
    pos = jnp.cumsum(onehot.astype(jnp.int32), axis=0) - 1
    counts = pos[-1] + 1
    padded = ((counts + tm - 1) // tm) * tm
    ends = jnp.cumsum(padded)
    starts = ends - padded
    dest = jnp.take_along_axis(starts[None, :] + pos, ti, axis=1)
    tile_start = jnp.arange(n_tiles, dtype=jnp.int32) * tm
    tile_expert = jnp.minimum(jnp.sum((tile_start[:, None] >= ends[None, :]).astype(jnp.int32), axis=1),
                              N_EXPERTS - 1)
    n_active = ends[-1] // tm
    pad_tile = jnp.where(padded > counts, ends // tm - 1, -1)
    tail_tile = n_tiles - 1 - jnp.arange(N_EXPERTS, dtype=jnp.int32)
    zero_tiles = jnp.concatenate([pad_tile, jnp.where(tail_tile >= n_active, tail_tile, -1)]).astype(jnp.int32)
    return dest.astype(jnp.int32), tile_expert, n_active.astype(jnp.int32).reshape(1), zero_tiles


def _dispatch_kernel(zt_ref, dest_ref, h_ref, xs_hbm, zbuf, sem, zsem):
    tt = h_ref.shape[0]
    tm = zbuf.shape[0]

    @pl.when(pl.program_id(0) == 0)
    def _():
        zbuf[...] = jnp.zeros_like(zbuf)

        def clear(k):
            start = pl.multiple_of(zt_ref[k] * tm, tm)
            return pltpu.make_async_copy(zbuf, xs_hbm.at[pl.ds(start, tm), :], zsem)

        for k in range(zt_ref.shape[0]):
            @pl.when(zt_ref[k] >= 0)
            def _():
                clear(k).start()
        for k in range(zt_ref.shape[0]):
            @pl.when(zt_ref[k] >= 0)
            def _():
                clear(k).wait()

    for r in range(tt):
        for slot in range(2):
            t = dest_ref[0, 0, 2 * r + slot]
            pltpu.make_async_copy(h_ref.at[pl.ds(r, 1), :], xs_hbm.at[pl.ds(t, 1), :], sem).start()
    for slot in range(2):
        pltpu.make_async_copy(h_ref, xs_hbm.at[pl.ds(0, tt), :], sem).wait()


def _moe_dispatch(h2d, dest, zero_tiles, n_pad):
    t, d = h2d.shape
    tt = min(TOK_TILE, t)
    grid_spec = pltpu.PrefetchScalarGridSpec(
        num_scalar_prefetch=1,
        grid=(t // tt,),
        in_specs=[pl.BlockSpec((1, 1, 2 * tt), lambda i, zt: (i, 0, 0), memory_space=pltpu.SMEM),
                  pl.BlockSpec((tt, d), lambda i, zt: (i, 0))],
        out_specs=pl.BlockSpec(memory_space=pl.ANY),
        scratch_shapes=[pltpu.VMEM((MOE_TILE, d), F32), pltpu.SemaphoreType.DMA(()),
                        pltpu.SemaphoreType.DMA(())])
    return pl.pallas_call(
        _dispatch_kernel,
        grid_spec=grid_spec,
        out_shape=jax.ShapeDtypeStruct((n_pad, d), F32),
        compiler_params=_params("arbitrary"),
        name="moe_dispatch",
    )(zero_tiles, dest.reshape(t // tt, 1, 2 * tt), h2d)


def _experts_kernel(te_ref, na_ref, x_ref, wg_ref, wu_ref, wd_ref, y_ref):
    active = pl.program_id(0) < na_ref[0]

    @pl.when(active)
    def _():
        x = x_ref[...].astype(BF16)
        g = _dot(x, wg_ref[...])
        u = _dot(x, wu_ref[...])
        y_ref[...] = _dot(((g * _sigmoid(g)) * u).astype(BF16), wd_ref[...])

    @pl.when(jnp.logical_not(active))
    def _():
        y_ref[...] = jnp.zeros_like(y_ref)


def _moe_experts(xs, tile_expert, n_active, wg, wu, wd):
    n_pad, d = xs.shape
    tm = MOE_TILE
    dff = wg.shape[2]
    once = pl.Buffered(1)
    grid_spec = pltpu.PrefetchScalarGridSpec(
        num_scalar_prefetch=2,
        grid=(n_pad // tm,),
        in_specs=[pl.BlockSpec((tm, d), lambda i, te, na: (i, 0)),
                  pl.BlockSpec((None, d, dff), lambda i, te, na: (te[i], 0, 0), pipeline_mode=once),
                  pl.BlockSpec((None, d, dff), lambda i, te, na: (te[i], 0, 0), pipeline_mode=once),
                  pl.BlockSpec((None, dff, d), lambda i, te, na: (te[i], 0, 0), pipeline_mode=once)],
        out_specs=pl.BlockSpec((tm, d), lambda i, te, na: (i, 0)))
    return pl.pallas_call(
        _experts_kernel,
        grid_spec=grid_spec,
        out_shape=jax.ShapeDtypeStruct((n_pad, d), F32),
        compiler_params=pltpu.CompilerParams(dimension_semantics=("arbitrary",),
                                             vmem_limit_bytes=ATTN_VMEM_LIMIT),
        name="moe_experts",
    )(tile_expert, n_active, xs, wg, wu, wd)


def _combine_kernel(dcur_ref, dnext_ref, ys_hbm, x_ref, tw_ref, mod_ref, g_ref, o_ref, buf, sem, *, final):
    i = pl.program_id(0)
    n = pl.num_programs(0)
    tt = x_ref.shape[0]

    def gather(d_ref, slot):
        for r in range(tt):
            for choice in range(2):
                t = d_ref[0, 0, 2 * r + choice]
                pltpu.make_async_copy(ys_hbm.at[pl.ds(t, 1), :], buf.at[slot, choice, pl.ds(r, 1), :],
                                      sem.at[slot]).start()

    @pl.when(i == 0)
    def _():
        gather(dcur_ref, 0)

    slot = i % 2
    for nxt in range(2):
        @pl.when(jnp.logical_and(i + 1 < n, slot == 1 - nxt))
        def _():
            gather(dnext_ref, nxt)

    for choice in range(2):
        pltpu.make_async_copy(ys_hbm.at[pl.ds(0, tt), :], buf.at[slot, choice], sem.at[slot]).wait()
    tw = tw_ref[...]
    y = tw[:, 0:1] * buf[slot, 0] + tw[:, 1:2] * buf[slot, 1]
    x = x_ref[...] + mod_ref[...][5:6] * y
    o_ref[...] = _rms(x, g_ref[...]) if final else x


def _moe_combine(x, ys, dest, tw, mod, final_g):
    b, s, d = x.shape
    t = b * s
    tt = min(256, s)
    n = t // tt
    per_b = s // tt
    dest3 = dest.reshape(n, 1, 2 * tt)
    return pl.pallas_call(
        functools.partial(_combine_kernel, final=final_g is not None),
        grid=(n,),
        in_specs=[pl.BlockSpec((1, 1, 2 * tt), lambda i: (i, 0, 0), memory_space=pltpu.SMEM),
                  pl.BlockSpec((1, 1, 2 * tt), lambda i: (jnp.minimum(i + 1, n - 1), 0, 0), memory_space=pltpu.SMEM),
                  pl.BlockSpec(memory_space=pl.ANY),
                  pl.BlockSpec((tt, d), lambda i: (i, 0)),
                  pl.BlockSpec((tt, 2), lambda i: (i, 0)),
                  pl.BlockSpec((None, 6, d), lambda i: (i // per_b, 0, 0)),
                  pl.BlockSpec((1, d), lambda i: (0, 0))],
        out_specs=pl.BlockSpec((tt, d), lambda i: (i, 0)),
        out_shape=jax.ShapeDtypeStruct((t, d), F32),
        scratch_shapes=[pltpu.VMEM((2, 2, tt, d), F32), pltpu.SemaphoreType.DMA((2,))],
        compiler_params=_params("arbitrary"),
        name="moe_combine",
    )(dest3, dest3, ys, x.reshape(t, d), tw, mod,
      (jnp.ones((d,), F32) if final_g is None else final_g).reshape(1, d)).reshape(b, s, d)


def kernel(x, c, positions, w_in, lru_conv_w, lru_conv_b, lru_w_a, lru_b_a, lru_w_x, lru_b_x, lru_lambda,
           s5_lambda_re, s5_lambda_im, s5_log_dt, s5_b_re, s5_b_im, s5_c_re, s5_c_im, s5_d, s5_glu_w,
           s5_glu_b, mix_gain, w_out, norm1_g, norm2_g, ada_w, ada_b, ffn_w_gate, ffn_w_up, ffn_w_down,
           router_w, moe_w_gate, moe_w_up, moe_w_down, final_g):
    b, s, d = x.shape
    depth = w_in.shape[0]
    cos_t, sin_t = _rope_tables(positions)
    mods = _adaln_mod(c, ada_w, ada_b).reshape(depth, b, 6, d)
    for l in range(depth):
        mod = mods[l]
        q, kaug, vaug, xr, gate, u = _inproj(x, mod, norm1_g[l], w_in[l].astype(BF16), cos_t, sin_t)
        o_att = _attention(q, kaug, vaug)
        wab = jnp.concatenate([_block_diag(lru_w_a[l]), _block_diag(lru_w_x[l])], axis=1).astype(BF16)
        bab = jnp.concatenate([lru_b_a[l], lru_b_x[l]]).reshape(1, -1)
        sp = jax.nn.softplus(-lru_lambda[l]).reshape(1, -1)
        o_lru = _lru(xr, gate, lru_conv_w[l], lru_conv_b[l].reshape(1, -1), wab, bab, sp)
        wb, a1, wc = _s5_tables(s5_lambda_re[l], s5_lambda_im[l], s5_log_dt[l], s5_b_re[l], s5_b_im[l],
                                s5_c_re[l], s5_c_im[l])
        o_s5 = _s5(u, wb, a1, wc, s5_d[l].reshape(1, -1), s5_glu_w[l].astype(BF16), s5_glu_b[l].reshape(1, -1))
        fg = final_g if l == depth - 1 else None
        e = l // 2
        if l % 2 == 0:
            x = _outproj_ffn(o_att, o_lru, o_s5, x, mod, mix_gain[l], w_out[l].astype(BF16), norm2_g[l],
                             ffn_w_gate[e].astype(BF16), ffn_w_up[e].astype(BF16), ffn_w_down[e].astype(BF16), fg)
        else:
            x, h, ti, tw = _outproj_moe(o_att, o_lru, o_s5, x, mod, mix_gain[l], w_out[l].astype(BF16),
                                        norm2_g[l], router_w[e])
            dest, tile_expert, n_active, zero_tiles = _moe_plan(ti.reshape(b * s, 2))
            xs = _moe_dispatch(h.reshape(b * s, d), dest, zero_tiles, tile_expert.shape[0] * MOE_TILE)
            ys = _moe_experts(xs, tile_expert, n_active, moe_w_gate[e].astype(BF16), moe_w_up[e].astype(BF16),
                              moe_w_down[e].astype(BF16))
            x = _moe_combine(x, ys, dest, tw.reshape(b * s, 2), mod, fg)
    return x
```

```python
import functools
import math

import jax
import jax.numpy as jnp
from jax import lax
from jax.experimental import pallas as pl
from jax.experimental.pallas import tpu as pltpu

F32 = jnp.float32
BF16 = jnp.bfloat16

HEAD_DIM = 64
ATT_WIDTH = 384
LRU_WIDTH = 384
LRU_BLOCKS = 6
S5_WIDTH = 256
S5_GROUPS = 16
S5_GROUP_CH = 16
S5_STATE = 64
S5_NS = S5_GROUPS * S5_STATE
LRU_C = 8.0
MOBA_BLOCK = 256
MOBA_TOPK = 3
N_EXPERTS = 8
EPS = 1e-6
NEG = -1e30
LOG2E = math.log2(math.e)
LANES = 128
SUBLANES = 8
VMEM_LIMIT = 48 * 1024 * 1024
ATTN_VMEM_LIMIT = 56 * 1024 * 1024

TOK_TILE = 512
MOE_TILE = 512


def _params(*sem):
    return pltpu.CompilerParams(dimension_semantics=sem, vmem_limit_bytes=VMEM_LIMIT)


def _dot(a, b):
    return jnp.dot(a, b, preferred_element_type=F32)


def _dot_t(a, b):
    return lax.dot_general(a, b, (((1,), (1,)), ((), ())), preferred_element_type=F32)


def _sigmoid(x):
    return 1.0 / (1.0 + jnp.exp(-x))


def _gelu(x):
    c = math.sqrt(2.0 / math.pi)
    return 0.5 * x * (1.0 + jnp.tanh(c * (x + 0.044715 * (x * x * x))))


def _rms(x, g):
    ms = jnp.mean(x * x, axis=-1, keepdims=True)
    return (x * lax.rsqrt(ms + EPS)) * g


def _mod_kernel(c_ref, w_ref, b_ref, o_ref):
    c = c_ref[...]
    cond = (c * _sigmoid(c)).astype(BF16)
    o_ref[...] = _dot(cond, w_ref[...].astype(BF16)) + b_ref[...]


def _adaln_mod(c, ada_w, ada_b):
    n_layers, d, n = ada_w.shape
    b = c.shape[0]
    nc = n // 4
    return pl.pallas_call(
        _mod_kernel,
        grid=(n_layers, n // nc),
        in_specs=[pl.BlockSpec((b, d), lambda l, j: (0, 0)),
                  pl.BlockSpec((None, d, nc), lambda l, j: (l, 0, j)),
                  pl.BlockSpec((None, 1, nc), lambda l, j: (l, 0, j))],
        out_specs=pl.BlockSpec((None, b, nc), lambda l, j: (l, 0, j)),
        out_shape=jax.ShapeDtypeStruct((n_layers, b, n), F32),
        compiler_params=_params("arbitrary", "arbitrary"),
        name="adaln_mod",
    )(c, ada_w, ada_b.reshape(n_layers, 1, n))


def _inproj_kernel(x_ref, mod_ref, g_ref, wl_ref, w_ref, cos_ref, sin_ref, cw_ref, cb_ref, wab_ref, bab_ref,
                   sp_ref, q_ref, k_ref, v_ref, ol_ref, u_ref, hist_ref, hcar_ref):
    mod = mod_ref[...]
    h = (_rms(x_ref[...], g_ref[...]) * (1.0 + mod[1:2]) + mod[0:1]).astype(BF16)
    proj_l = _dot(h, wl_ref[...])
    _lru_tile(proj_l[:, :LRU_WIDTH], proj_l[:, LRU_WIDTH:], cw_ref, cb_ref, wab_ref, bab_ref, sp_ref,
              ol_ref, hist_ref, hcar_ref)
    proj = _dot(h, w_ref[...])
    cos = cos_ref[...]
    sin = sin_ref[...]
    lane = lax.broadcasted_iota(jnp.int32, cos.shape, 1)
    first_half = (lane % HEAD_DIM) < (HEAD_DIM // 2)

    def rope(t):
        outs = []
        for i in range(ATT_WIDTH // LANES):
            xi = t[:, LANES * i:LANES * (i + 1)]
            partner = jnp.where(first_half,
                                pltpu.roll(xi, LANES - HEAD_DIM // 2, 1),
                                pltpu.roll(xi, HEAD_DIM // 2, 1))
            outs.append(xi * cos + partner * sin)
        return jnp.concatenate(outs, axis=1)

    a = ATT_WIDTH
    q_ref[...] = (rope(proj[:, 0:a]) * (HEAD_DIM ** -0.5 * LOG2E)).astype(BF16)
    k = rope(proj[:, a:2 * a]).astype(BF16)
    v = proj[:, 2 * a:3 * a].astype(BF16)
    tt = k.shape[0]
    blk_id = (pl.program_id(1) * tt + lax.broadcasted_iota(jnp.int32, (tt, LANES), 0)) // MOBA_BLOCK
    onehot = jnp.where(blk_id == lane, 1.0, 0.0).astype(BF16)
    ones = jnp.ones((tt, LANES), BF16)
    for i in range(a // LANES):
        k_ref[:, 2 * LANES * i:2 * LANES * i + LANES] = k[:, LANES * i:LANES * (i + 1)]
        k_ref[:, 2 * LANES * i + LANES:2 * LANES * (i + 1)] = onehot
        v_ref[:, 2 * LANES * i:2 * LANES * i + LANES] = v[:, LANES * i:LANES * (i + 1)]
        v_ref[:, 2 * LANES * i + LANES:2 * LANES * (i + 1)] = ones
    u_ref[...] = proj[:, 3 * a:]


def _inproj(x, mod, g, w_in_bf16, cos_t, sin_t, conv_w, conv_b, wab_bf16, bab, sp):
    b, s, d = x.shape
    tt = min(TOK_TILE, s)
    a = 3 * ATT_WIDTH
    w_lru = w_in_bf16[:, a:a + 2 * LRU_WIDTH]
    w_rest = jnp.concatenate([w_in_bf16[:, :a], w_in_bf16[:, a + 2 * LRU_WIDTH:]], axis=1)
    tok = lambda w: pl.BlockSpec((None, tt, w), lambda i, j: (i, j, 0))
    full = lambda arr: pl.BlockSpec(arr.shape, lambda i, j: (0,) * arr.ndim)
    g2 = g.reshape(1, d)
    outs = [jax.ShapeDtypeStruct((b, s, ATT_WIDTH), BF16)] + [jax.ShapeDtypeStruct((b, s, 2 * ATT_WIDTH), BF16)] * 2 + [
        jax.ShapeDtypeStruct((b, s, LRU_WIDTH), F32),
        jax.ShapeDtypeStruct((b, s, S5_WIDTH), F32)]
    return pl.pallas_call(
        _inproj_kernel,
        grid=(b, s // tt),
        in_specs=[tok(d), pl.BlockSpec((None, 6, d), lambda i, j: (i, 0, 0)), full(g2),
                  full(w_lru), full(w_rest), tok(LANES), tok(LANES),
                  full(conv_w), full(conv_b), full(wab_bf16), full(bab), full(sp)],
        out_specs=[tok(ATT_WIDTH), tok(2 * ATT_WIDTH), tok(2 * ATT_WIDTH), tok(LRU_WIDTH), tok(S5_WIDTH)],
        out_shape=outs,
        scratch_shapes=[pltpu.VMEM((8 + tt, LRU_WIDTH), F32), pltpu.VMEM((1, LRU_WIDTH), F32)],
        compiler_params=_params("arbitrary", "arbitrary"),
        name="inproj_lru",
    )(x, mod, g2, w_lru, w_rest, cos_t, sin_t, conv_w, conv_b, wab_bf16, bab, sp)


def _attn_kernel(q_ref, k_ref, v_ref, o_ref, kmean_ref, s_ref, mrun_ref, acc_ref, *, n_blk, n_pair):
    j = pl.program_id(1)
    blk = MOBA_BLOCK
    rows = 2 * blk
    kw = 2 * LANES

    @pl.when(j == 0)
    def _():
        for n in range(n_blk):
            for p in range(n_pair):
                kb = k_ref[n * blk:(n + 1) * blk, kw * p:kw * p + LANES]
                kmean_ref[n:n + 1, LANES * p:LANES * (p + 1)] = jnp.mean(kb.astype(F32), axis=0, keepdims=True)

    lane = lax.broadcasted_iota(jnp.int32, (blk, LANES), 1)
    bid = lax.broadcasted_iota(jnp.int32, (n_blk, rows), 0)
    valid = bid < j
    place = jnp.where(lax.broadcasted_iota(jnp.int32, (n_blk, LANES), 0)
                      == lax.broadcasted_iota(jnp.int32, (n_blk, LANES), 1), 1.0, 0.0).astype(BF16)
    own = pl.multiple_of(j * blk, blk)
    qpos = lax.broadcasted_iota(jnp.int32, (rows, blk), 0) % blk
    kpos = lax.broadcasted_iota(jnp.int32, (rows, blk), 1)
    causal = kpos <= qpos

    q_aug = []
    s_own = []
    for p in range(n_pair):
        q = q_ref[:, LANES * p:LANES * (p + 1)]
        zero = jnp.zeros_like(q)
        q2 = jnp.concatenate([jnp.where(lane < HEAD_DIM, q, zero),
                              jnp.where(lane >= HEAD_DIM, q, zero)], axis=0)
        kmean = kmean_ref[:, LANES * p:LANES * (p + 1)]
        km_hi = kmean.astype(BF16)
        km_lo = (kmean - km_hi.astype(F32)).astype(BF16)
        gw = jnp.where(valid, _dot_t(km_hi, q2) + _dot_t(km_lo, q2), NEG)
        sel = jnp.zeros(gw.shape, dtype=jnp.bool_)
        for _ in range(MOBA_TOPK):
            mx = jnp.max(gw, axis=0, keepdims=True)
            idx = jnp.min(jnp.where(gw == mx, bid, n_blk), axis=0, keepdims=True)
            pick = bid == idx
            sel = jnp.logical_or(sel, pick)
            gw = jnp.where(pick, -jnp.inf, gw)
        bias_t = jnp.where(jnp.logical_and(sel, valid), 0.0, NEG).astype(BF16)
        bias = lax.dot_general(bias_t, place, (((0,), (0,)), ((), ())), preferred_element_type=F32)
        q_aug.append(jnp.concatenate([q2, bias.astype(BF16)], axis=1))
        so = jnp.where(causal, _dot_t(q2, k_ref[pl.ds(own, blk), kw * p:kw * p + LANES]), NEG)
        s_own.append(so)
        mrun_ref[p] = so

    n_two = (j + 1) // 2

    def walk(body):
        def pair_of(i, carry):
            body(2 * i)
            body(2 * i + 1)
            return carry
        lax.fori_loop(0, n_two // 2, pair_of, 0)

        @pl.when(n_two % 2 == 1)
        def _():
            body(n_two - 1)

    def scores(i):
        start = pl.multiple_of(i * (2 * blk), 2 * blk)
        for p in range(n_pair):
            sn = _dot_t(q_aug[p], k_ref[pl.ds(start, 2 * blk), kw * p:kw * (p + 1)])
            s_ref[p, i] = sn
            mrun_ref[p] = jnp.maximum(mrun_ref[p], jnp.maximum(sn[:, :blk], sn[:, blk:]))

    walk(scores)

    m = []
    for p in range(n_pair):
        mp = jnp.max(mrun_ref[p], axis=1, keepdims=True)
        m.append(mp)
        acc_ref[p] = _dot(jnp.exp2(s_own[p] - mp).astype(BF16), v_ref[pl.ds(own, blk), kw * p:kw * (p + 1)])

    def values(i):
        start = pl.multiple_of(i * (2 * blk), 2 * blk)
        for p in range(n_pair):
            pn = jnp.exp2(s_ref[p, i] - m[p]).astype(BF16)
            acc_ref[p] += _dot(pn, v_ref[pl.ds(start, 2 * blk), kw * p:kw * (p + 1)])

    walk(values)
    for p in range(n_pair):
        acc = acc_ref[p]
        out = acc[:, :LANES] / acc[:, LANES:]
        o_ref[:, LANES * p:LANES * (p + 1)] = jnp.where(lane < HEAD_DIM, out[:blk], out[blk:])


def _attention(q, kaug, vaug):
    b, s, w = q.shape
    n_blk = s // MOBA_BLOCK
    n_pair = w // LANES
    rows = 2 * MOBA_BLOCK
    once = pl.Buffered(1)
    return pl.pallas_call(
        functools.partial(_attn_kernel, n_blk=n_blk, n_pair=n_pair),
        grid=(b, n_blk),
        in_specs=[pl.BlockSpec((None, MOBA_BLOCK, w), lambda i, j: (i, j, 0)),
                  pl.BlockSpec((None, s, 2 * w), lambda i, j: (i, 0, 0), pipeline_mode=once),
                  pl.BlockSpec((None, s, 2 * w), lambda i, j: (i, 0, 0), pipeline_mode=once)],
        out_specs=pl.BlockSpec((None, MOBA_BLOCK, w), lambda i, j: (i, j, 0)),
        out_shape=jax.ShapeDtypeStruct((b, s, w), F32),
        scratch_shapes=[pltpu.VMEM((n_blk, w), F32),
                        pltpu.VMEM((n_pair, n_blk // 2, rows, 2 * MOBA_BLOCK), F32),
                        pltpu.VMEM((n_pair, rows, MOBA_BLOCK), F32),
                        pltpu.VMEM((n_pair, rows, 2 * LANES), F32)],
        compiler_params=pltpu.CompilerParams(dimension_semantics=("arbitrary", "arbitrary"),
                                             vmem_limit_bytes=ATTN_VMEM_LIMIT),
        name="moba_attention",
    )(q, kaug, vaug)


def _lru_tile(xr, gate, cw_ref, cb_ref, wab_ref, bab_ref, sp_ref, o_ref, hist_ref, h_ref):
    lt = xr.shape[0]
    w = LRU_WIDTH

    @pl.when(pl.program_id(1) == 0)
    def _():
        hist_ref[0:8, :] = jnp.zeros((8, w), F32)
        h_ref[...] = jnp.zeros_like(h_ref)

    hist_ref[8:8 + lt, :] = xr
    cw = cw_ref[...]
    n_tap = cw.shape[0]
    xc = cb_ref[...] + cw[n_tap - 1:n_tap] * xr
    for back in range(1, n_tap):
        xc = xc + cw[n_tap - 1 - back:n_tap - back] * hist_ref[8 - back:8 - back + lt, :]
    hist_ref[0:8, :] = xr[lt - 8:lt]

    ri = _sigmoid(_dot(xc.astype(BF16), wab_ref[...]) + bab_ref[...])
    r = ri[:, :w]
    i = ri[:, w:]
    log_a = (-LRU_C) * r * sp_ref[...]
    a = jnp.exp(log_a)
    expm1 = jnp.tanh(log_a) * (a * a + 1.0)
    u = jnp.sqrt(jnp.maximum(-expm1, 0.0)) * (i * xc)

    n_grp = lt // SUBLANES
    u = u.reshape(n_grp, SUBLANES, w)
    a = a.reshape(n_grp, SUBLANES, w)
    row = lax.broadcasted_iota(jnp.int32, (SUBLANES, w), 0)
    for k in range(3):
        d = 1 << k
        inside = row >= d
        u = u + jnp.where(inside, a, 0.0) * pltpu.roll(u, d, 1)
        a = a * jnp.where(inside, pltpu.roll(a, d, 1), 1.0)
    gg = _gelu(gate)
    carry = h_ref[...]
    for g in range(n_grp):
        rows = slice(g * SUBLANES, (g + 1) * SUBLANES)
        hg = u[g] + a[g] * carry
        o_ref[rows, :] = hg * gg[rows]
        carry = hg[SUBLANES - 1:SUBLANES]
    h_ref[...] = carry


S5_STEPS = 64


def _s5_kernel(u_ref, wb_ref, a_ref, wc_ref, d_ref, gw_ref, gb_ref, o_ref, xs_ref, bu_ref, st_ref):
    nb, lt, w = u_ref.shape
    ns = S5_NS
    rows = nb * lt

    @pl.when(pl.program_id(0) == 0)
    def _():
        xs_ref[...] = jnp.zeros_like(xs_ref)

    u = u_ref[...].reshape(rows, w)
    r = lax.broadcasted_iota(jnp.int32, (rows, rows), 0)
    c = lax.broadcasted_iota(jnp.int32, (rows, rows), 1)
    to_tm = jnp.where((r % nb) * lt + r // nb == c, 1.0, 0.0).astype(BF16)
    to_bt = jnp.where((r % lt) * nb + r // lt == c, 1.0, 0.0).astype(BF16)
    u_tm = _dot(to_tm, u.astype(BF16)).astype(BF16)
    bu_ref[...] = _dot(u_tm, wb_ref[...]).reshape(lt, nb, 2 * ns)
    ar = jnp.broadcast_to(a_ref[0:1, :], (nb, ns))
    ai = jnp.broadcast_to(a_ref[1:2, :], (nb, ns))

    def step(t, carry):
        re, im = carry
        b = bu_ref[t]
        re, im = ar * re - ai * im + b[:, :ns], ar * im + ai * re + b[:, ns:]
        st_ref[t] = jnp.concatenate([re, im], axis=1)
        return re, im

    x0 = xs_ref[...]
    carry = (x0[:, :ns], x0[:, ns:])
    for t in range(lt):
        carry = step(t, carry)
    re, im = carry
    xs_ref[...] = jnp.concatenate([re, im], axis=1)
    y_tm = _dot(st_ref[...].reshape(rows, 2 * ns).astype(BF16), wc_ref[...])
    y_hi = y_tm.astype(BF16)
    y_lo = (y_tm - y_hi.astype(F32)).astype(BF16)
    y_bt = _dot(to_bt, y_hi) + _dot(to_bt, y_lo)
    y = _gelu(y_bt + d_ref[...] * u)
    z = _sigmoid(_dot(y.astype(BF16), gw_ref[...]) + gb_ref[...])
    o_ref[...] = (y * z).reshape(nb, lt, w)


def _s5(u, wb_bf16, a1, wc_bf16, d_skip, glu_w_bf16, glu_b):
    b, s, w = u.shape
    lt = min(S5_STEPS, s)
    tok = pl.BlockSpec((b, lt, w), lambda i: (0, i, 0))
    full = lambda a: pl.BlockSpec(a.shape, lambda i: (0,) * a.ndim)
    return pl.pallas_call(
        _s5_kernel,
        grid=(s // lt,),
        in_specs=[tok, full(wb_bf16), full(a1), full(wc_bf16), full(d_skip), full(glu_w_bf16), full(glu_b)],
        out_specs=tok,
        out_shape=jax.ShapeDtypeStruct((b, s, w), F32),
        scratch_shapes=[pltpu.VMEM((b, 2 * S5_NS), F32), pltpu.VMEM((lt, b, 2 * S5_NS), F32),
                        pltpu.VMEM((lt, b, 2 * S5_NS), F32)],
        compiler_params=_params("arbitrary"),
        name="s5",
    )(u, wb_bf16, a1, wc_bf16, d_skip, glu_w_bf16, glu_b)


def _s5_tables(lam_re, lam_im, log_dt, b_re, b_im, c_re, c_im):
    g, p, h = b_re.shape
    dt = jnp.exp(log_dt)[:, None]
    mag = jnp.exp(lam_re * dt)
    ab_re, ab_im = mag * jnp.cos(lam_im * dt), mag * jnp.sin(lam_im * dt)
    den = lam_re * lam_re + lam_im * lam_im
    nr, ni = ab_re - 1.0, ab_im
    co_re = (nr * lam_re + ni * lam_im) / den
    co_im = (ni * lam_re - nr * lam_im) / den
    bb_re = co_re[..., None] * b_re - co_im[..., None] * b_im
    bb_im = co_re[..., None] * b_im + co_im[..., None] * b_re
    eye = jnp.eye(g, dtype=F32)
    wb = jnp.concatenate([jnp.einsum('gph,gk->ghkp', bb_re, eye).reshape(g * h, g * p),
                          jnp.einsum('gph,gk->ghkp', bb_im, eye).reshape(g * h, g * p)], axis=1)
    wc = jnp.concatenate([jnp.einsum('ghp,gk->kpgh', c_re, eye).reshape(g * p, g * h),
                          -jnp.einsum('ghp,gk->kpgh', c_im, eye).reshape(g * p, g * h)], axis=0)

    a1 = jnp.stack([ab_re.reshape(g * p), ab_im.reshape(g * p)])
    return wb.astype(BF16), a1, wc.astype(BF16)


def _mix_residual(oa_ref, ol_ref, os_ref, x_ref, mod, mg_ref, w_ref, g2_ref):
    mg = mg_ref[...]
    a, l = ATT_WIDTH, LRU_WIDTH
    o = (_dot(_rms(oa_ref[...], mg[:, :a]).astype(BF16), w_ref[:a, :])
         + _dot(_rms(ol_ref[...], mg[:, a:a + l]).astype(BF16), w_ref[a:a + l, :])
         + _dot(_rms(os_ref[...], mg[:, a + l:]).astype(BF16), w_ref[a + l:, :]))
    x = x_ref[...] + mod[2:3] * o
    h = _rms(x, g2_ref[...]) * (1.0 + mod[4:5]) + mod[3:4]
    return x, h


def _outproj_ffn_kernel(oa_ref, ol_ref, os_ref, x_ref, mod_ref, mg_ref, w_ref, g2_ref,
                        wg_ref, wu_ref, wd_ref, fg_ref, o_ref, *, final):
    mod = mod_ref[...]
    x, h = _mix_residual(oa_ref, ol_ref, os_ref, x_ref, mod, mg_ref, w_ref, g2_ref)
    hb = h.astype(BF16)
    g = _dot(hb, wg_ref[...])
    u = _dot(hb, wu_ref[...])
    f = _dot(((g * _sigmoid(g)) * u).astype(BF16), wd_ref[...])
    x = x + mod[5:6] * f
    o_ref[...] = _rms(x, fg_ref[...]) if final else x


def _outproj_moe_kernel(oa_ref, ol_ref, os_ref, x_ref, mod_ref, mg_ref, w_ref, g2_ref, rwt_ref,
                        x_out, h_out, ti_out, tw_out):
    x, h = _mix_residual(oa_ref, ol_ref, os_ref, x_ref, mod_ref[...], mg_ref, w_ref, g2_ref)
    x_out[...] = x
    h_out[...] = h
    hb = h.astype(BF16)
    h_lo = (h - hb.astype(F32)).astype(BF16)
    rwt = rwt_ref[...]
    rw_hi = rwt.astype(BF16)
    rw_lo = (rwt - rw_hi.astype(F32)).astype(BF16)
    logits = _dot_t(rw_hi, hb) + _dot_t(rw_lo, hb) + _dot_t(rw_hi, h_lo)
    eid = lax.broadcasted_iota(jnp.int32, logits.shape, 0)
    m1 = jnp.max(logits, axis=0, keepdims=True)
    i1 = jnp.min(jnp.where(logits == m1, eid, N_EXPERTS), axis=0, keepdims=True)
    rest_l = jnp.where(eid == i1, -jnp.inf, logits)
    m2 = jnp.max(rest_l, axis=0, keepdims=True)
    i2 = jnp.min(jnp.where(rest_l == m2, eid, N_EXPERTS), axis=0, keepdims=True)
    e2 = jnp.exp(m2 - m1)
    den = 1.0 + e2
    ti_out[...] = jnp.concatenate([i1, i2], axis=0)
    tw_out[...] = jnp.concatenate([1.0 / den, e2 / den], axis=0)


def _outproj_specs(x, tt):
    b, s, d = x.shape
    tok = lambda w: pl.BlockSpec((None, tt, w), lambda i, j: (i, j, 0))
    once = lambda r, c: pl.BlockSpec((r, c), lambda i, j: (0, 0), pipeline_mode=pl.Buffered(1))
    specs = [tok(ATT_WIDTH), tok(LRU_WIDTH), tok(S5_WIDTH), tok(d),
             pl.BlockSpec((None, 6, d), lambda i, j: (i, 0, 0)), once(1, d), once(d, d), once(1, d)]
    return tok, once, specs


def _outproj_ffn(o_att, o_lru, o_s5, x, mod, mix_gain, w_out_bf16, g2, wg, wu, wd, final_g):
    b, s, d = x.shape
    tt = min(TOK_TILE, s)
    dff = wg.shape[1]
    tok, once, specs = _outproj_specs(x, tt)
    fg = jnp.ones((d,), F32) if final_g is None else final_g
    return pl.pallas_call(
        functools.partial(_outproj_ffn_kernel, final=final_g is not None),
        grid=(b, s // tt),
        in_specs=specs + [once(d, dff), once(d, dff), once(dff, d), once(1, d)],
        out_specs=tok(d),
        out_shape=jax.ShapeDtypeStruct((b, s, d), F32),
        compiler_params=pltpu.CompilerParams(dimension_semantics=("arbitrary", "arbitrary"),
                                             vmem_limit_bytes=ATTN_VMEM_LIMIT),
        name="outproj_ffn",
    )(o_att, o_lru, o_s5, x, mod, mix_gain.reshape(1, d), w_out_bf16, g2.reshape(1, d), wg, wu, wd,
      fg.reshape(1, d))


def _outproj_moe(o_att, o_lru, o_s5, x, mod, mix_gain, w_out_bf16, g2, router_w):
    b, s, d = x.shape
    tt = min(TOK_TILE, s)
    tok, once, specs = _outproj_specs(x, tt)
    top = pl.BlockSpec((None, 2, tt), lambda i, j: (i, 0, j))
    x_new, h, ti, tw = pl.pallas_call(
        _outproj_moe_kernel,
        grid=(b, s // tt),
        in_specs=specs + [once(N_EXPERTS, d)],
        out_specs=[tok(d), tok(d), top, top],
        out_shape=[jax.ShapeDtypeStruct((b, s, d), F32), jax.ShapeDtypeStruct((b, s, d), F32),
                   jax.ShapeDtypeStruct((b, 2, s), jnp.int32), jax.ShapeDtypeStruct((b, 2, s), F32)],
        compiler_params=_params("arbitrary", "arbitrary"),
        name="outproj_moe",
    )(o_att, o_lru, o_s5, x, mod, mix_gain.reshape(1, d), w_out_bf16, g2.reshape(1, d), router_w.T)
    return x_new, h, jnp.swapaxes(ti, 1, 2), jnp.swapaxes(tw, 1, 2)


def _rope_tables(positions):
    half = HEAD_DIM // 2
    inv = 10000.0 ** (-jnp.arange(0, HEAD_DIM, 2, dtype=F32) / HEAD_DIM)
    ang = positions.astype(F32)[..., None] * inv
    cos, sin = jnp.cos(ang), jnp.sin(ang)
    reps = LANES // HEAD_DIM
    cos_t = jnp.tile(jnp.concatenate([cos, cos], axis=-1), (1, 1, reps))
    sin_t = jnp.tile(jnp.concatenate([-sin, sin], axis=-1), (1, 1, reps))
    return cos_t, sin_t


def _block_diag(w):
    h, i, j = w.shape
    eye = jnp.eye(h, dtype=w.dtype)
    return jnp.einsum('hij,hk->hikj', w, eye).reshape(h * i, h * j)


def _moe_plan(ti):
    t = ti.shape[0]
    tm = MOE_TILE
    n_tiles = (2 * t) // tm + N_EXPERTS
    onehot = jnp.any(ti[:, :, None] == jnp.arange(N_EXPERTS, dtype=jnp.int32), axis=1)
    pos = jnp.cumsum(onehot.astype(jnp.int32), axis=0) - 1
    counts = pos[-1] + 1
    padded = ((counts + tm - 1) // tm) * tm
    ends = jnp.cumsum(padded)
    starts = ends - padded
    dest = jnp.take_along_axis(starts[None, :] + pos, ti, axis=1)
    tile_start = jnp.arange(n_tiles, dtype=jnp.int32) * tm
    tile_expert = jnp.minimum(jnp.sum((tile_start[:, None] >= ends[None, :]).astype(jnp.int32), axis=1),
                              N_EXPERTS - 1)
    n_active = ends[-1] // tm
    pad_tile = jnp.where(padded > counts, ends // tm - 1, -1)
    tail_tile = n_tiles - 1 - jnp.arange(N_EXPERTS, dtype=jnp.int32)
    zero_tiles = jnp.concatenate([pad_tile, jnp.where(tail_tile >= n_active, tail_tile, -1)]).astype(jnp.int32)
    return dest.astype(jnp.int32), tile_expert, n_active.astype(jnp.int32).reshape(1), zero_tiles


def _dispatch_kernel(zt_ref, dest_ref, h_ref, xs_hbm, zbuf, sem, zsem):
    tt = h_ref.shape[0]
    tm = zbuf.shape[0]

    @pl.when(pl.program_id(0) == 0)
    def _():
        zbuf[...] = jnp.zeros_like(zbuf)

        def clear(k):
            start = pl.multiple_of(zt_ref[k] * tm, tm)
            return pltpu.make_async_copy(zbuf, xs_hbm.at[pl.ds(start, tm), :], zsem)

        for k in range(zt_ref.shape[0]):
            @pl.when(zt_ref[k] >= 0)
            def _():
                clear(k).start()
        for k in range(zt_ref.shape[0]):
            @pl.when(zt_ref[k] >= 0)
            def _():
                clear(k).wait()

    for r in range(tt):
        for slot in range(2):
            t = dest_ref[0, 0, 2 * r + slot]
            pltpu.make_async_copy(h_ref.at[pl.ds(r, 1), :], xs_hbm.at[pl.ds(t, 1), :], sem).start()
    for slot in range(2):
        pltpu.make_async_copy(h_ref, xs_hbm.at[pl.ds(0, tt), :], sem).wait()


def _moe_dispatch(h2d, dest, zero_tiles, n_pad):
    t, d = h2d.shape
    tt = min(TOK_TILE, t)
    grid_spec = pltpu.PrefetchScalarGridSpec(
        num_scalar_prefetch=1,
        grid=(t // tt,),
        in_specs=[pl.BlockSpec((1, 1, 2 * tt), lambda i, zt: (i, 0, 0), memory_space=pltpu.SMEM),
                  pl.BlockSpec((tt, d), lambda i, zt: (i, 0))],
        out_specs=pl.BlockSpec(memory_space=pl.ANY),
        scratch_shapes=[pltpu.VMEM((MOE_TILE, d), F32), pltpu.SemaphoreType.DMA(()),
                        pltpu.SemaphoreType.DMA(())])
    return pl.pallas_call(
        _dispatch_kernel,
        grid_spec=grid_spec,
        out_shape=jax.ShapeDtypeStruct((n_pad, d), F32),
        compiler_params=_params("arbitrary"),
        name="moe_dispatch",
    )(zero_tiles, dest.reshape(t // tt, 1, 2 * tt), h2d)


def _experts_kernel(te_ref, na_ref, x_ref, wg_ref, wu_ref, wd_ref, y_ref):
    active = pl.program_id(0) < na_ref[0]

    @pl.when(active)
    def _():
        x = x_ref[...].astype(BF16)
        g = _dot(x, wg_ref[...])
        u = _dot(x, wu_ref[...])
        y_ref[...] = _dot(((g * _sigmoid(g)) * u).astype(BF16), wd_ref[...])

    @pl.when(jnp.logical_not(active))
    def _():
        y_ref[...] = jnp.zeros_like(y_ref)


def _moe_experts(xs, tile_expert, n_active, wg, wu, wd):
    n_pad, d = xs.shape
    tm = MOE_TILE
    dff = wg.shape[2]
    once = pl.Buffered(1)
    grid_spec = pltpu.PrefetchScalarGridSpec(
        num_scalar_prefetch=2,
        grid=(n_pad // tm,),
        in_specs=[pl.BlockSpec((tm, d), lambda i, te, na: (i, 0)),
                  pl.BlockSpec((None, d, dff), lambda i, te, na: (te[i], 0, 0), pipeline_mode=once),
                  pl.BlockSpec((None, d, dff), lambda i, te, na: (te[i], 0, 0), pipeline_mode=once),
                  pl.BlockSpec((None, dff, d), lambda i, te, na: (te[i], 0, 0), pipeline_mode=once)],
        out_specs=pl.BlockSpec((tm, d), lambda i, te, na: (i, 0)))
    return pl.pallas_call(
        _experts_kernel,
        grid_spec=grid_spec,
        out_shape=jax.ShapeDtypeStruct((n_pad, d), F32),
        compiler_params=pltpu.CompilerParams(dimension_semantics=("arbitrary",),
                                             vmem_limit_bytes=ATTN_VMEM_LIMIT),
        name="moe_experts",
    )(tile_expert, n_active, xs, wg, wu, wd)


def _combine_kernel(dcur_ref, dnext_ref, ys_hbm, x_ref, tw_ref, mod_ref, g_ref, o_ref, buf, sem, *, final):
    i = pl.program_id(0)
    n = pl.num_programs(0)
    tt = x_ref.shape[0]

    def gather(d_ref, slot):
        for r in range(tt):
            for choice in range(2):
                t = d_ref[0, 0, 2 * r + choice]
                pltpu.make_async_copy(ys_hbm.at[pl.ds(t, 1), :], buf.at[slot, choice, pl.ds(r, 1), :],
                                      sem.at[slot]).start()

    @pl.when(i == 0)
    def _():
        gather(dcur_ref, 0)

    slot = i % 2
    for nxt in range(2):
        @pl.when(jnp.logical_and(i + 1 < n, slot == 1 - nxt))
        def _():
            gather(dnext_ref, nxt)

    for choice in range(2):
        pltpu.make_async_copy(ys_hbm.at[pl.ds(0, tt), :], buf.at[slot, choice], sem.at[slot]).wait()
    tw = tw_ref[...]
    y = tw[:, 0:1] * buf[slot, 0] + tw[:, 1:2] * buf[slot, 1]
    x = x_ref[...] + mod_ref[...][5:6] * y
    o_ref[...] = _rms(x, g_ref[...]) if final else x


def _moe_combine(x, ys, dest, tw, mod, final_g):
    b, s, d = x.shape
    t = b * s
    tt = min(TOK_TILE, s)
    n = t // tt
    per_b = s // tt
    dest3 = dest.reshape(n, 1, 2 * tt)
    return pl.pallas_call(
        functools.partial(_combine_kernel, final=final_g is not None),
        grid=(n,),
        in_specs=[pl.BlockSpec((1, 1, 2 * tt), lambda i: (i, 0, 0), memory_space=pltpu.SMEM),
                  pl.BlockSpec((1, 1, 2 * tt), lambda i: (jnp.minimum(i + 1, n - 1), 0, 0), memory_space=pltpu.SMEM),
                  pl.BlockSpec(memory_space=pl.ANY),
                  pl.BlockSpec((tt, d), lambda i: (i, 0)),
                  pl.BlockSpec((tt, 2), lambda i: (i, 0)),
                  pl.BlockSpec((None, 6, d), lambda i: (i // per_b, 0, 0)),
                  pl.BlockSpec((1, d), lambda i: (0, 0))],
        out_specs=pl.BlockSpec((tt, d), lambda i: (i, 0)),
        out_shape=jax.ShapeDtypeStruct((t, d), F32),
        scratch_shapes=[pltpu.VMEM((2, 2, tt, d), F32), pltpu.SemaphoreType.DMA((2,))],
        compiler_params=_params("arbitrary"),
        name="moe_combine",
    )(dest3, dest3, ys, x.reshape(t, d), tw, mod,
      (jnp.ones((d,), F32) if final_g is None else final_g).reshape(1, d)).reshape(b, s, d)


def kernel(x, c, positions, w_in, lru_conv_w, lru_conv_b, lru_w_a, lru_b_a, lru_w_x, lru_b_x, lru_lambda,
           s5_lambda_re, s5_lambda_im, s5_log_dt, s5_b_re, s5_b_im, s5_c_re, s5_c_im, s5_d, s5_glu_w,
           s5_glu_b, mix_gain, w_out, norm1_g, norm2_g, ada_w, ada_b, ffn_w_gate, ffn_w_up, ffn_w_down,
           router_w, moe_w_gate, moe_w_up, moe_w_down, final_g):
    b, s, d = x.shape
    depth = w_in.shape[0]
    cos_t, sin_t = _rope_tables(positions)
    mods = _adaln_mod(c, ada_w, ada_b).reshape(depth, b, 6, d)
    for l in range(depth):
        mod = mods[l]
        wab = jnp.concatenate([_block_diag(lru_w_a[l]), _block_diag(lru_w_x[l])], axis=1).astype(BF16)
        bab = jnp.concatenate([lru_b_a[l], lru_b_x[l]]).reshape(1, -1)
        sp = jax.nn.softplus(-lru_lambda[l]).reshape(1, -1)
        q, kaug, vaug, o_lru, u = _inproj(x, mod, norm1_g[l], w_in[l].astype(BF16), cos_t, sin_t,
                                          lru_conv_w[l], lru_conv_b[l].reshape(1, -1), wab, bab, sp)
        o_att = _attention(q, kaug, vaug)
        wb, a1, wc = _s5_tables(s5_lambda_re[l], s5_lambda_im[l], s5_log_dt[l], s5_b_re[l], s5_b_im[l],
                                s5_c_re[l], s5_c_im[l])
        o_s5 = _s5(u, wb, a1, wc, s5_d[l].reshape(1, -1), s5_glu_w[l].astype(BF16), s5_glu_b[l].reshape(1, -1))
        fg = final_g if l == depth - 1 else None
        e = l // 2
        if l % 2 == 0:
            x = _outproj_ffn(o_att, o_lru, o_s5, x, mod, mix_gain[l], w_out[l].astype(BF16), norm2_g[l],
                             ffn_w_gate[e].astype(BF16), ffn_w_up[e].astype(BF16), ffn_w_down[e].astype(BF16), fg)
        else:
            x, h, ti, tw = _outproj_moe(o_att, o_lru, o_s5, x, mod, mix_gain[l], w_out[l].astype(BF16),
                                        norm2_g[l], router_w[e])
            dest, tile_expert, n_active, zero_tiles = _moe_plan(ti.reshape(b * s, 2))
            xs = _moe_dispatch(h.reshape(b * s, d), dest, zero_tiles, tile_expert.shape[0] * MOE_TILE)
            ys = _moe_experts(xs, tile_expert, n_active, moe_w_gate[e].astype(BF16), moe_w_up[e].astype(BF16),
                              moe_w_down[e].astype(BF16))
            x = _moe_combine(x, ys, dest, tw.reshape(b * s, 2), mod, fg)
    return x
```

```python
import functools
import math

import jax
import jax.numpy as jnp
from jax import lax
from jax.experimental import pallas as pl
from jax.experimental.pallas import tpu as pltpu

F32 = jnp.float32
BF16 = jnp.bfloat16

HEAD_DIM = 64
ATT_WIDTH = 384
LRU_WIDTH = 384
LRU_BLOCKS = 6
S5_WIDTH = 256
S5_GROUPS = 16
S5_GROUP_CH = 16
S5_STATE = 64
S5_NS = S5_GROUPS * S5_STATE
LRU_C = 8.0
MOBA_BLOCK = 256
MOBA_TOPK = 3
N_EXPERTS = 8
EPS = 1e-6
NEG = -1e30
LOG2E = math.log2(math.e)
LANES = 128
SUBLANES = 8
VMEM_LIMIT = 48 * 1024 * 1024
ATTN_VMEM_LIMIT = 56 * 1024 * 1024

TOK_TILE = 512
MOE_TILE = 512


def _params(*sem):
    return pltpu.CompilerParams(dimension_semantics=sem, vmem_limit_bytes=VMEM_LIMIT)


def _dot(a, b):
    return jnp.dot(a, b, preferred_element_type=F32)


def _dot_t(a, b):
    return lax.dot_general(a, b, (((1,), (1,)), ((), ())), preferred_element_type=F32)


def _sigmoid(x):
    return 1.0 / (1.0 + jnp.exp(-x))


def _gelu(x):
    c = math.sqrt(2.0 / math.pi)
    return 0.5 * x * (1.0 + jnp.tanh(c * (x + 0.044715 * (x * x * x))))


def _rms(x, g):
    ms = jnp.mean(x * x, axis=-1, keepdims=True)
    return (x * lax.rsqrt(ms + EPS)) * g


def _mod_kernel(c_ref, w_ref, b_ref, o_ref):
    c = c_ref[...]
    cond = (c * _sigmoid(c)).astype(BF16)
    o_ref[...] = _dot(cond, w_ref[...].astype(BF16)) + b_ref[...]


def _adaln_mod(c, ada_w, ada_b):
    n_layers, d, n = ada_w.shape
    b = c.shape[0]
    nc = n // 4
    return pl.pallas_call(
        _mod_kernel,
        grid=(n_layers, n // nc),
        in_specs=[pl.BlockSpec((b, d), lambda l, j: (0, 0)),
                  pl.BlockSpec((None, d, nc), lambda l, j: (l, 0, j)),
                  pl.BlockSpec((None, 1, nc), lambda l, j: (l, 0, j))],
        out_specs=pl.BlockSpec((None, b, nc), lambda l, j: (l, 0, j)),
        out_shape=jax.ShapeDtypeStruct((n_layers, b, n), F32),
        compiler_params=_params("arbitrary", "arbitrary"),
        name="adaln_mod",
    )(c, ada_w, ada_b.reshape(n_layers, 1, n))


def _inproj_kernel(x_ref, mod_ref, g_ref, wl_ref, w_ref, cos_ref, sin_ref, cw_ref, cb_ref, wab_ref, bab_ref,
                   sp_ref, q_ref, k_ref, v_ref, ol_ref, u_ref, hist_ref, hcar_ref):
    mod = mod_ref[...]
    h = (_rms(x_ref[...], g_ref[...]) * (1.0 + mod[1:2]) + mod[0:1]).astype(BF16)
    proj_l = _dot(h, wl_ref[...])

    @pl.when(pl.program_id(1) == 0)
    def _():
        hist_ref[0:8, :] = jnp.zeros((8, LRU_WIDTH), F32)
        hcar_ref[...] = jnp.zeros_like(hcar_ref)

    _lru_tile(proj_l[:, :LRU_WIDTH], proj_l[:, LRU_WIDTH:], cw_ref, cb_ref, wab_ref, bab_ref, sp_ref,
              ol_ref, hist_ref, hcar_ref)
    proj = _dot(h, w_ref[...])
    cos = cos_ref[...]
    sin = sin_ref[...]
    lane = lax.broadcasted_iota(jnp.int32, cos.shape, 1)
    first_half = (lane % HEAD_DIM) < (HEAD_DIM // 2)

    def rope(t):
        outs = []
        for i in range(ATT_WIDTH // LANES):
            xi = t[:, LANES * i:LANES * (i + 1)]
            partner = jnp.where(first_half,
                                pltpu.roll(xi, LANES - HEAD_DIM // 2, 1),
                                pltpu.roll(xi, HEAD_DIM // 2, 1))
            outs.append(xi * cos + partner * sin)
        return jnp.concatenate(outs, axis=1)

    a = ATT_WIDTH
    q_ref[...] = (rope(proj[:, 0:a]) * (HEAD_DIM ** -0.5 * LOG2E)).astype(BF16)
    k = rope(proj[:, a:2 * a]).astype(BF16)
    v = proj[:, 2 * a:3 * a].astype(BF16)
    tt = k.shape[0]
    blk_id = (pl.program_id(1) * tt + lax.broadcasted_iota(jnp.int32, (tt, LANES), 0)) // MOBA_BLOCK
    onehot = jnp.where(blk_id == lane, 1.0, 0.0).astype(BF16)
    ones = jnp.ones((tt, LANES), BF16)
    for i in range(a // LANES):
        k_ref[:, 2 * LANES * i:2 * LANES * i + LANES] = k[:, LANES * i:LANES * (i + 1)]
        k_ref[:, 2 * LANES * i + LANES:2 * LANES * (i + 1)] = onehot
        v_ref[:, 2 * LANES * i:2 * LANES * i + LANES] = v[:, LANES * i:LANES * (i + 1)]
        v_ref[:, 2 * LANES * i + LANES:2 * LANES * (i + 1)] = ones
    u_ref[...] = proj[:, 3 * a:]


def _inproj(x, mod, g, w_in_bf16, cos_t, sin_t, conv_w, conv_b, wab_bf16, bab, sp):
    b, s, d = x.shape
    tt = min(TOK_TILE, s)
    a = 3 * ATT_WIDTH
    w_lru = w_in_bf16[:, a:a + 2 * LRU_WIDTH]
    w_rest = jnp.concatenate([w_in_bf16[:, :a], w_in_bf16[:, a + 2 * LRU_WIDTH:]], axis=1)
    tok = lambda w: pl.BlockSpec((None, tt, w), lambda i, j: (i, j, 0))
    full = lambda arr: pl.BlockSpec(arr.shape, lambda i, j: (0,) * arr.ndim)
    g2 = g.reshape(1, d)
    outs = [jax.ShapeDtypeStruct((b, s, ATT_WIDTH), BF16)] + [jax.ShapeDtypeStruct((b, s, 2 * ATT_WIDTH), BF16)] * 2 + [
        jax.ShapeDtypeStruct((b, s, LRU_WIDTH), F32),
        jax.ShapeDtypeStruct((b, s, S5_WIDTH), F32)]
    return pl.pallas_call(
        _inproj_kernel,
        grid=(b, s // tt),
        in_specs=[tok(d), pl.BlockSpec((None, 6, d), lambda i, j: (i, 0, 0)), full(g2),
                  full(w_lru), full(w_rest), tok(LANES), tok(LANES),
                  full(conv_w), full(conv_b), full(wab_bf16), full(bab), full(sp)],
        out_specs=[tok(ATT_WIDTH), tok(2 * ATT_WIDTH), tok(2 * ATT_WIDTH), tok(LRU_WIDTH), tok(S5_WIDTH)],
        out_shape=outs,
        scratch_shapes=[pltpu.VMEM((8 + tt, LRU_WIDTH), F32), pltpu.VMEM((1, LRU_WIDTH), F32)],
        compiler_params=_params("arbitrary", "arbitrary"),
        name="inproj_lru",
    )(x, mod, g2, w_lru, w_rest, cos_t, sin_t, conv_w, conv_b, wab_bf16, bab, sp)


def _attn_kernel(q_ref, k_ref, v_ref, o_ref, kmean_ref, s_ref, mrun_ref, acc_ref, *, n_blk, n_pair):
    j = pl.program_id(1)
    blk = MOBA_BLOCK
    rows = 2 * blk
    kw = 2 * LANES

    @pl.when(j == 0)
    def _():
        for n in range(n_blk):
            for p in range(n_pair):
                kb = k_ref[n * blk:(n + 1) * blk, kw * p:kw * p + LANES]
                kmean_ref[n:n + 1, LANES * p:LANES * (p + 1)] = jnp.mean(kb.astype(F32), axis=0, keepdims=True)

    lane = lax.broadcasted_iota(jnp.int32, (blk, LANES), 1)
    bid = lax.broadcasted_iota(jnp.int32, (n_blk, rows), 0)
    valid = bid < j
    place = jnp.where(lax.broadcasted_iota(jnp.int32, (n_blk, LANES), 0)
                      == lax.broadcasted_iota(jnp.int32, (n_blk, LANES), 1), 1.0, 0.0).astype(BF16)
    own = pl.multiple_of(j * blk, blk)
    qpos = lax.broadcasted_iota(jnp.int32, (rows, blk), 0) % blk
    kpos = lax.broadcasted_iota(jnp.int32, (rows, blk), 1)
    causal = kpos <= qpos

    q_aug = []
    s_own = []
    for p in range(n_pair):
        q = q_ref[:, LANES * p:LANES * (p + 1)]
        zero = jnp.zeros_like(q)
        q2 = jnp.concatenate([jnp.where(lane < HEAD_DIM, q, zero),
                              jnp.where(lane >= HEAD_DIM, q, zero)], axis=0)
        kmean = kmean_ref[:, LANES * p:LANES * (p + 1)]
        km_hi = kmean.astype(BF16)
        km_lo = (kmean - km_hi.astype(F32)).astype(BF16)
        gw = jnp.where(valid, _dot_t(km_hi, q2) + _dot_t(km_lo, q2), NEG)
        sel = jnp.zeros(gw.shape, dtype=jnp.bool_)
        for _ in range(MOBA_TOPK):
            mx = jnp.max(gw, axis=0, keepdims=True)
            idx = jnp.min(jnp.where(gw == mx, bid, n_blk), axis=0, keepdims=True)
            pick = bid == idx
            sel = jnp.logical_or(sel, pick)
            gw = jnp.where(pick, -jnp.inf, gw)
        bias_t = jnp.where(jnp.logical_and(sel, valid), 0.0, NEG).astype(BF16)
        bias = lax.dot_general(bias_t, place, (((0,), (0,)), ((), ())), preferred_element_type=F32)
        q_aug.append(jnp.concatenate([q2, bias.astype(BF16)], axis=1))
        so = jnp.where(causal, _dot_t(q2, k_ref[pl.ds(own, blk), kw * p:kw * p + LANES]), NEG)
        s_own.append(so)
        mrun_ref[p] = jnp.maximum(so[:, :LANES], so[:, LANES:])

    n_two = (j + 1) // 2

    def walk(body):
        def pair_of(i, carry):
            body(2 * i)
            body(2 * i + 1)
            return carry
        lax.fori_loop(0, n_two // 2, pair_of, 0)

        @pl.when(n_two % 2 == 1)
        def _():
            body(n_two - 1)

    def scores(i):
        start = pl.multiple_of(i * (2 * blk), 2 * blk)
        for p in range(n_pair):
            sn = _dot_t(q_aug[p], k_ref[pl.ds(start, 2 * blk), kw * p:kw * (p + 1)])
            s_ref[p, i] = sn
            top = jnp.maximum(jnp.maximum(sn[:, :LANES], sn[:, LANES:2 * LANES]),
                              jnp.maximum(sn[:, 2 * LANES:3 * LANES], sn[:, 3 * LANES:]))
            mrun_ref[p] = jnp.maximum(mrun_ref[p], top)

    walk(scores)

    m = []
    for p in range(n_pair):
        mp = jnp.max(mrun_ref[p], axis=1, keepdims=True)
        m.append(mp)
        acc_ref[p] = _dot(jnp.exp2(s_own[p] - mp).astype(BF16), v_ref[pl.ds(own, blk), kw * p:kw * (p + 1)])

    def values(i):
        start = pl.multiple_of(i * (2 * blk), 2 * blk)
        for p in range(n_pair):
            pn = jnp.exp2(s_ref[p, i] - m[p]).astype(BF16)
            acc_ref[p] += _dot(pn, v_ref[pl.ds(start, 2 * blk), kw * p:kw * (p + 1)])

    walk(values)
    for p in range(n_pair):
        acc = acc_ref[p]
        out = acc[:, :LANES] / acc[:, LANES:]
        o_ref[:, LANES * p:LANES * (p + 1)] = jnp.where(lane < HEAD_DIM, out[:blk], out[blk:])


def _attention(q, kaug, vaug):
    b, s, w = q.shape
    n_blk = s // MOBA_BLOCK
    n_pair = w // LANES
    rows = 2 * MOBA_BLOCK
    once = pl.Buffered(1)
    return pl.pallas_call(
        functools.partial(_attn_kernel, n_blk=n_blk, n_pair=n_pair),
        grid=(b, n_blk),
        in_specs=[pl.BlockSpec((None, MOBA_BLOCK, w), lambda i, j: (i, j, 0)),
                  pl.BlockSpec((None, s, 2 * w), lambda i, j: (i, 0, 0), pipeline_mode=once),
                  pl.BlockSpec((None, s, 2 * w), lambda i, j: (i, 0, 0), pipeline_mode=once)],
        out_specs=pl.BlockSpec((None, MOBA_BLOCK, w), lambda i, j: (i, j, 0)),
        out_shape=jax.ShapeDtypeStruct((b, s, w), F32),
        scratch_shapes=[pltpu.VMEM((n_blk, w), F32),
                        pltpu.VMEM((n_pair, n_blk // 2, rows, 2 * MOBA_BLOCK), F32),
                        pltpu.VMEM((n_pair, rows, LANES), F32),
                        pltpu.VMEM((n_pair, rows, 2 * LANES), F32)],
        compiler_params=pltpu.CompilerParams(dimension_semantics=("arbitrary", "arbitrary"),
                                             vmem_limit_bytes=ATTN_VMEM_LIMIT),
        name="moba_attention",
    )(q, kaug, vaug)


def _lru_tile(xr, gate, cw_ref, cb_ref, wab_ref, bab_ref, sp_ref, o_ref, hist_ref, h_ref):
    lt = xr.shape[0]
    w = LRU_WIDTH
    hist_ref[8:8 + lt, :] = xr
    cw = cw_ref[...]
    n_tap = cw.shape[0]
    xc = cb_ref[...] + cw[n_tap - 1:n_tap] * xr
    for back in range(1, n_tap):
        xc = xc + cw[n_tap - 1 - back:n_tap - back] * hist_ref[8 - back:8 - back + lt, :]
    hist_ref[0:8, :] = xr[lt - 8:lt]

    ri = _sigmoid(_dot(xc.astype(BF16), wab_ref[...]) + bab_ref[...])
    r = ri[:, :w]
    i = ri[:, w:]
    log_a = (-LRU_C) * r * sp_ref[...]
    a = jnp.exp(log_a)
    expm1 = jnp.tanh(log_a) * (a * a + 1.0)
    u = jnp.sqrt(jnp.maximum(-expm1, 0.0)) * (i * xc)

    n_grp = lt // SUBLANES
    u = u.reshape(n_grp, SUBLANES, w)
    a = a.reshape(n_grp, SUBLANES, w)
    row = lax.broadcasted_iota(jnp.int32, (SUBLANES, w), 0)
    for k in range(3):
        d = 1 << k
        inside = row >= d
        u = u + jnp.where(inside, a, 0.0) * pltpu.roll(u, d, 1)
        a = a * jnp.where(inside, pltpu.roll(a, d, 1), 1.0)
    gg = _gelu(gate)
    carry = h_ref[...]
    for g in range(n_grp):
        rows = slice(g * SUBLANES, (g + 1) * SUBLANES)
        hg = u[g] + a[g] * carry
        o_ref[rows, :] = hg * gg[rows]
        carry = hg[SUBLANES - 1:SUBLANES]
    h_ref[...] = carry


S5_STEPS = 64


def _s5_kernel(u_ref, wb_ref, a_ref, wc_ref, d_ref, gw_ref, gb_ref, o_ref, xs_ref, bu_ref, st_ref):
    nb, lt, w = u_ref.shape
    ns = S5_NS
    rows = nb * lt

    @pl.when(pl.program_id(0) == 0)
    def _():
        xs_ref[...] = jnp.zeros_like(xs_ref)

    u = u_ref[...].reshape(rows, w)
    r = lax.broadcasted_iota(jnp.int32, (rows, rows), 0)
    c = lax.broadcasted_iota(jnp.int32, (rows, rows), 1)
    to_tm = jnp.where((r % nb) * lt + r // nb == c, 1.0, 0.0).astype(BF16)
    to_bt = jnp.where((r % lt) * nb + r // lt == c, 1.0, 0.0).astype(BF16)
    u_tm = _dot(to_tm, u.astype(BF16)).astype(BF16)
    bu_ref[...] = _dot(u_tm, wb_ref[...]).reshape(lt, nb, 2 * ns)
    ar = jnp.broadcast_to(a_ref[0:1, :], (nb, ns))
    ai = jnp.broadcast_to(a_ref[1:2, :], (nb, ns))

    def step(t, carry):
        re, im = carry
        b = bu_ref[t]
        re, im = ar * re - ai * im + b[:, :ns], ar * im + ai * re + b[:, ns:]
        st_ref[t] = jnp.concatenate([re, im], axis=1)
        return re, im

    x0 = xs_ref[...]
    carry = (x0[:, :ns], x0[:, ns:])
    for t in range(lt):
        carry = step(t, carry)
    re, im = carry
    xs_ref[...] = jnp.concatenate([re, im], axis=1)
    y_tm = _dot(st_ref[...].reshape(rows, 2 * ns).astype(BF16), wc_ref[...])
    y_hi = y_tm.astype(BF16)
    y_lo = (y_tm - y_hi.astype(F32)).astype(BF16)
    y_bt = _dot(to_bt, y_hi) + _dot(to_bt, y_lo)
    y = _gelu(y_bt + d_ref[...] * u)
    z = _sigmoid(_dot(y.astype(BF16), gw_ref[...]) + gb_ref[...])
    o_ref[...] = (y * z).reshape(nb, lt, w)


def _s5(u, wb_bf16, a1, wc_bf16, d_skip, glu_w_bf16, glu_b):
    b, s, w = u.shape
    lt = min(S5_STEPS, s)
    tok = pl.BlockSpec((b, lt, w), lambda i: (0, i, 0))
    full = lambda a: pl.BlockSpec(a.shape, lambda i: (0,) * a.ndim)
    return pl.pallas_call(
        _s5_kernel,
        grid=(s // lt,),
        in_specs=[tok, full(wb_bf16), full(a1), full(wc_bf16), full(d_skip), full(glu_w_bf16), full(glu_b)],
        out_specs=tok,
        out_shape=jax.ShapeDtypeStruct((b, s, w), F32),
        scratch_shapes=[pltpu.VMEM((b, 2 * S5_NS), F32), pltpu.VMEM((lt, b, 2 * S5_NS), F32),
                        pltpu.VMEM((lt, b, 2 * S5_NS), F32)],
        compiler_params=_params("arbitrary"),
        name="s5",
    )(u, wb_bf16, a1, wc_bf16, d_skip, glu_w_bf16, glu_b)


def _s5_tables(lam_re, lam_im, log_dt, b_re, b_im, c_re, c_im):
    g, p, h = b_re.shape
    dt = jnp.exp(log_dt)[:, None]
    mag = jnp.exp(lam_re * dt)
    ab_re, ab_im = mag * jnp.cos(lam_im * dt), mag * jnp.sin(lam_im * dt)
    den = lam_re * lam_re + lam_im * lam_im
    nr, ni = ab_re - 1.0, ab_im
    co_re = (nr * lam_re + ni * lam_im) / den
    co_im = (ni * lam_re - nr * lam_im) / den
    bb_re = co_re[..., None] * b_re - co_im[..., None] * b_im
    bb_im = co_re[..., None] * b_im + co_im[..., None] * b_re
    eye = jnp.eye(g, dtype=F32)
    wb = jnp.concatenate([jnp.einsum('gph,gk->ghkp', bb_re, eye).reshape(g * h, g * p),
                          jnp.einsum('gph,gk->ghkp', bb_im, eye).reshape(g * h, g * p)], axis=1)
    wc = jnp.concatenate([jnp.einsum('ghp,gk->kpgh', c_re, eye).reshape(g * p, g * h),
                          -jnp.einsum('ghp,gk->kpgh', c_im, eye).reshape(g * p, g * h)], axis=0)

    a1 = jnp.stack([ab_re.reshape(g * p), ab_im.reshape(g * p)])
    return wb.astype(BF16), a1, wc.astype(BF16)


def _mix_residual(oa_ref, ol_ref, os_ref, x_ref, mod, mg_ref, w_ref, g2_ref):
    mg = mg_ref[...]
    a, l = ATT_WIDTH, LRU_WIDTH
    o = (_dot(_rms(oa_ref[...], mg[:, :a]).astype(BF16), w_ref[:a, :])
         + _dot(_rms(ol_ref[...], mg[:, a:a + l]).astype(BF16), w_ref[a:a + l, :])
         + _dot(_rms(os_ref[...], mg[:, a + l:]).astype(BF16), w_ref[a + l:, :]))
    x = x_ref[...] + mod[2:3] * o
    h = _rms(x, g2_ref[...]) * (1.0 + mod[4:5]) + mod[3:4]
    return x, h


def _outproj_ffn_kernel(oa_ref, ol_ref, os_ref, x_ref, mod_ref, mg_ref, w_ref, g2_ref,
                        wg_ref, wu_ref, wd_ref, fg_ref, o_ref, *, final):
    mod = mod_ref[...]
    x, h = _mix_residual(oa_ref, ol_ref, os_ref, x_ref, mod, mg_ref, w_ref, g2_ref)
    hb = h.astype(BF16)
    g = _dot(hb, wg_ref[...])
    u = _dot(hb, wu_ref[...])
    f = _dot(((g * _sigmoid(g)) * u).astype(BF16), wd_ref[...])
    x = x + mod[5:6] * f
    o_ref[...] = _rms(x, fg_ref[...]) if final else x


def _outproj_moe_kernel(oa_ref, ol_ref, os_ref, x_ref, mod_ref, mg_ref, w_ref, g2_ref, rwt_ref,
                        x_out, h_out, ti_out, tw_out):
    x, h = _mix_residual(oa_ref, ol_ref, os_ref, x_ref, mod_ref[...], mg_ref, w_ref, g2_ref)
    x_out[...] = x
    h_out[...] = h
    hb = h.astype(BF16)
    h_lo = (h - hb.astype(F32)).astype(BF16)
    rwt = rwt_ref[...]
    rw_hi = rwt.astype(BF16)
    rw_lo = (rwt - rw_hi.astype(F32)).astype(BF16)
    logits = _dot_t(rw_hi, hb) + _dot_t(rw_lo, hb) + _dot_t(rw_hi, h_lo)
    eid = lax.broadcasted_iota(jnp.int32, logits.shape, 0)
    m1 = jnp.max(logits, axis=0, keepdims=True)
    i1 = jnp.min(jnp.where(logits == m1, eid, N_EXPERTS), axis=0, keepdims=True)
    rest_l = jnp.where(eid == i1, -jnp.inf, logits)
    m2 = jnp.max(rest_l, axis=0, keepdims=True)
    i2 = jnp.min(jnp.where(rest_l == m2, eid, N_EXPERTS), axis=0, keepdims=True)
    e2 = jnp.exp(m2 - m1)
    den = 1.0 + e2
    ti_out[...] = jnp.concatenate([i1, i2], axis=0)
    tw_out[...] = jnp.concatenate([1.0 / den, e2 / den], axis=0)


def _outproj_specs(x, tt):
    b, s, d = x.shape
    tok = lambda w: pl.BlockSpec((None, tt, w), lambda i, j: (i, j, 0))
    once = lambda r, c: pl.BlockSpec((r, c), lambda i, j: (0, 0), pipeline_mode=pl.Buffered(1))
    specs = [tok(ATT_WIDTH), tok(LRU_WIDTH), tok(S5_WIDTH), tok(d),
             pl.BlockSpec((None, 6, d), lambda i, j: (i, 0, 0)), once(1, d), once(d, d), once(1, d)]
    return tok, once, specs


def _outproj_ffn(o_att, o_lru, o_s5, x, mod, mix_gain, w_out_bf16, g2, wg, wu, wd, final_g):
    b, s, d = x.shape
    tt = min(TOK_TILE, s)
    dff = wg.shape[1]
    tok, once, specs = _outproj_specs(x, tt)
    fg = jnp.ones((d,), F32) if final_g is None else final_g
    return pl.pallas_call(
        functools.partial(_outproj_ffn_kernel, final=final_g is not None),
        grid=(b, s // tt),
        in_specs=specs + [once(d, dff), once(d, dff), once(dff, d), once(1, d)],
        out_specs=tok(d),
        out_shape=jax.ShapeDtypeStruct((b, s, d), F32),
        compiler_params=pltpu.CompilerParams(dimension_semantics=("arbitrary", "arbitrary"),
                                             vmem_limit_bytes=ATTN_VMEM_LIMIT),
        name="outproj_ffn",
    )(o_att, o_lru, o_s5, x, mod, mix_gain.reshape(1, d), w_out_bf16, g2.reshape(1, d), wg, wu, wd,
      fg.reshape(1, d))


def _outproj_moe(o_att, o_lru, o_s5, x, mod, mix_gain, w_out_bf16, g2, router_w):
    b, s, d = x.shape
    tt = min(TOK_TILE, s)
    tok, once, specs = _outproj_specs(x, tt)
    top = pl.BlockSpec((None, 2, tt), lambda i, j: (i, 0, j))
    x_new, h, ti, tw = pl.pallas_call(
        _outproj_moe_kernel,
        grid=(b, s // tt),
        in_specs=specs + [once(N_EXPERTS, d)],
        out_specs=[tok(d), tok(d), top, top],
        out_shape=[jax.ShapeDtypeStruct((b, s, d), F32), jax.ShapeDtypeStruct((b, s, d), F32),
                   jax.ShapeDtypeStruct((b, 2, s), jnp.int32), jax.ShapeDtypeStruct((b, 2, s), F32)],
        compiler_params=_params("arbitrary", "arbitrary"),
        name="outproj_moe",
    )(o_att, o_lru, o_s5, x, mod, mix_gain.reshape(1, d), w_out_bf16, g2.reshape(1, d), router_w.T)
    return x_new, h, jnp.swapaxes(ti, 1, 2), jnp.swapaxes(tw, 1, 2)


def _rope_tables(positions):
    half = HEAD_DIM // 2
    inv = 10000.0 ** (-jnp.arange(0, HEAD_DIM, 2, dtype=F32) / HEAD_DIM)
    ang = positions.astype(F32)[..., None] * inv
    cos, sin = jnp.cos(ang), jnp.sin(ang)
    reps = LANES // HEAD_DIM
    cos_t = jnp.tile(jnp.concatenate([cos, cos], axis=-1), (1, 1, reps))
    sin_t = jnp.tile(jnp.concatenate([-sin, sin], axis=-1), (1, 1, reps))
    return cos_t, sin_t


def _block_diag(w):
    h, i, j = w.shape
    eye = jnp.eye(h, dtype=w.dtype)
    return jnp.einsum('hij,hk->hikj', w, eye).reshape(h * i, h * j)


def _moe_plan(ti):
    t = ti.shape[0]
    tm = MOE_TILE
    n_tiles = (2 * t) // tm + N_EXPERTS
    onehot = jnp.any(ti[:, :, None] == jnp.arange(N_EXPERTS, dtype=jnp.int32), axis=1)
    pos = jnp.cumsum(onehot.astype(jnp.int32), axis=0) - 1
    counts = pos[-1] + 1
    padded = ((counts + tm - 1) // tm) * tm
    ends = jnp.cumsum(padded)
    starts = ends - padded
    dest = jnp.take_along_axis(starts[None, :] + pos, ti, axis=1)
    tile_start = jnp.arange(n_tiles, dtype=jnp.int32) * tm
    tile_expert = jnp.minimum(jnp.sum((tile_start[:, None] >= ends[None, :]).astype(jnp.int32), axis=1),
                              N_EXPERTS - 1)
    n_active = ends[-1] // tm
    pad_tile = jnp.where(padded > counts, ends // tm - 1, -1)
    tail_tile = n_tiles - 1 - jnp.arange(N_EXPERTS, dtype=jnp.int32)
    zero_tiles = jnp.concatenate([pad_tile, jnp.where(tail_tile >= n_active, tail_tile, -1)]).astype(jnp.int32)
    return dest.astype(jnp.int32), tile_expert, n_active.astype(jnp.int32).reshape(1), zero_tiles


def _dispatch_kernel(zt_ref, dest_ref, h_hbm, xs_hbm, hbuf, zbuf, lsem, ssem, zsem):
    i = pl.program_id(0)
    n = pl.num_programs(0)
    tt = hbuf.shape[1]
    tm = zbuf.shape[0]

    def load(step, slot):
        start = pl.multiple_of(step * tt, tt)
        return pltpu.make_async_copy(h_hbm.at[pl.ds(start, tt), :], hbuf.at[slot], lsem.at[slot])

    def wait_rows(slot):
        for _ in range(2):
            pltpu.make_async_copy(hbuf.at[slot], xs_hbm.at[pl.ds(0, tt), :], ssem.at[slot]).wait()

    @pl.when(i == 0)
    def _():
        load(0, 0).start()
        zbuf[...] = jnp.zeros_like(zbuf)

        def clear(k):
            start = pl.multiple_of(zt_ref[k] * tm, tm)
            return pltpu.make_async_copy(zbuf, xs_hbm.at[pl.ds(start, tm), :], zsem)

        for k in range(zt_ref.shape[0]):
            @pl.when(zt_ref[k] >= 0)
            def _():
                clear(k).start()
        for k in range(zt_ref.shape[0]):
            @pl.when(zt_ref[k] >= 0)
            def _():
                clear(k).wait()

    for slot in range(2):
        @pl.when(i % 2 == slot)
        def _():
            @pl.when(i >= 1)
            def _():
                wait_rows(1 - slot)

            @pl.when(i + 1 < n)
            def _():
                load(i + 1, 1 - slot).start()

            load(i, slot).wait()
            for r in range(tt):
                for choice in range(2):
                    t = dest_ref[0, 0, 2 * r + choice]
                    pltpu.make_async_copy(hbuf.at[slot, pl.ds(r, 1), :], xs_hbm.at[pl.ds(t, 1), :],
                                          ssem.at[slot]).start()

            @pl.when(i == n - 1)
            def _():
                wait_rows(slot)


def _moe_dispatch(h2d, dest, zero_tiles, n_pad):
    t, d = h2d.shape
    tt = min(TOK_TILE, t)
    grid_spec = pltpu.PrefetchScalarGridSpec(
        num_scalar_prefetch=1,
        grid=(t // tt,),
        in_specs=[pl.BlockSpec((1, 1, 2 * tt), lambda i, zt: (i, 0, 0), memory_space=pltpu.SMEM),
                  pl.BlockSpec(memory_space=pl.ANY)],
        out_specs=pl.BlockSpec(memory_space=pl.ANY),
        scratch_shapes=[pltpu.VMEM((2, tt, d), F32), pltpu.VMEM((MOE_TILE, d), F32),
                        pltpu.SemaphoreType.DMA((2,)), pltpu.SemaphoreType.DMA((2,)),
                        pltpu.SemaphoreType.DMA(())])
    return pl.pallas_call(
        _dispatch_kernel,
        grid_spec=grid_spec,
        out_shape=jax.ShapeDtypeStruct((n_pad, d), F32),
        compiler_params=_params("arbitrary"),
        name="moe_dispatch",
    )(zero_tiles, dest.reshape(t // tt, 1, 2 * tt), h2d)


def _experts_kernel(te_ref, na_ref, x_ref, wg_ref, wu_ref, wd_ref, y_ref):
    active = pl.program_id(0) < na_ref[0]

    @pl.when(active)
    def _():
        x = x_ref[...].astype(BF16)
        g = _dot(x, wg_ref[...])
        u = _dot(x, wu_ref[...])
        y_ref[...] = _dot(((g * _sigmoid(g)) * u).astype(BF16), wd_ref[...])

    @pl.when(jnp.logical_not(active))
    def _():
        y_ref[...] = jnp.zeros_like(y_ref)


def _moe_experts(xs, tile_expert, n_active, wg, wu, wd):
    n_pad, d = xs.shape
    tm = MOE_TILE
    dff = wg.shape[2]
    once = pl.Buffered(1)
    grid_spec = pltpu.PrefetchScalarGridSpec(
        num_scalar_prefetch=2,
        grid=(n_pad // tm,),
        in_specs=[pl.BlockSpec((tm, d), lambda i, te, na: (i, 0)),
                  pl.BlockSpec((None, d, dff), lambda i, te, na: (te[i], 0, 0), pipeline_mode=once),
                  pl.BlockSpec((None, d, dff), lambda i, te, na: (te[i], 0, 0), pipeline_mode=once),
                  pl.BlockSpec((None, dff, d), lambda i, te, na: (te[i], 0, 0), pipeline_mode=once)],
        out_specs=pl.BlockSpec((tm, d), lambda i, te, na: (i, 0)))
    return pl.pallas_call(
        _experts_kernel,
        grid_spec=grid_spec,
        out_shape=jax.ShapeDtypeStruct((n_pad, d), F32),
        compiler_params=pltpu.CompilerParams(dimension_semantics=("arbitrary",),
                                             vmem_limit_bytes=ATTN_VMEM_LIMIT),
        name="moe_experts",
    )(tile_expert, n_active, xs, wg, wu, wd)


def _combine_kernel(dcur_ref, dnext_ref, ys_hbm, x_ref, tw_ref, mod_ref, g_ref, o_ref, buf, sem, *, final):
    i = pl.program_id(0)
    n = pl.num_programs(0)
    tt = x_ref.shape[0]

    def gather(d_ref, slot):
        for r in range(tt):
            for choice in range(2):
                t = d_ref[0, 0, 2 * r + choice]
                pltpu.make_async_copy(ys_hbm.at[pl.ds(t, 1), :], buf.at[slot, choice, pl.ds(r, 1), :],
                                      sem.at[slot]).start()

    @pl.when(i == 0)
    def _():
        gather(dcur_ref, 0)

    slot = i % 2
    for nxt in range(2):
        @pl.when(jnp.logical_and(i + 1 < n, slot == 1 - nxt))
        def _():
            gather(dnext_ref, nxt)

    for choice in range(2):
        pltpu.make_async_copy(ys_hbm.at[pl.ds(0, tt), :], buf.at[slot, choice], sem.at[slot]).wait()
    tw = tw_ref[...]
    y = tw[:, 0:1] * buf[slot, 0] + tw[:, 1:2] * buf[slot, 1]
    x = x_ref[...] + mod_ref[...][5:6] * y
    o_ref[...] = _rms(x, g_ref[...]) if final else x


def _moe_combine(x, ys, dest, tw, mod, final_g):
    b, s, d = x.shape
    t = b * s
    tt = min(TOK_TILE, s)
    n = t // tt
    per_b = s // tt
    dest3 = dest.reshape(n, 1, 2 * tt)
    return pl.pallas_call(
        functools.partial(_combine_kernel, final=final_g is not None),
        grid=(n,),
        in_specs=[pl.BlockSpec((1, 1, 2 * tt), lambda i: (i, 0, 0), memory_space=pltpu.SMEM),
                  pl.BlockSpec((1, 1, 2 * tt), lambda i: (jnp.minimum(i + 1, n - 1), 0, 0), memory_space=pltpu.SMEM),
                  pl.BlockSpec(memory_space=pl.ANY),
                  pl.BlockSpec((tt, d), lambda i: (i, 0)),
                  pl.BlockSpec((tt, 2), lambda i: (i, 0)),
                  pl.BlockSpec((None, 6, d), lambda i: (i // per_b, 0, 0)),
                  pl.BlockSpec((1, d), lambda i: (0, 0))],
        out_specs=pl.BlockSpec((tt, d), lambda i: (i, 0)),
        out_shape=jax.ShapeDtypeStruct((t, d), F32),
        scratch_shapes=[pltpu.VMEM((2, 2, tt, d), F32), pltpu.SemaphoreType.DMA((2,))],
        compiler_params=_params("arbitrary"),
        name="moe_combine",
    )(dest3, dest3, ys, x.reshape(t, d), tw, mod,
      (jnp.ones((d,), F32) if final_g is None else final_g).reshape(1, d)).reshape(b, s, d)


def kernel(x, c, positions, w_in, lru_conv_w, lru_conv_b, lru_w_a, lru_b_a, lru_w_x, lru_b_x, lru_lambda,
           s5_lambda_re, s5_lambda_im, s5_log_dt, s5_b_re, s5_b_im, s5_c_re, s5_c_im, s5_d, s5_glu_w,
           s5_glu_b, mix_gain, w_out, norm1_g, norm2_g, ada_w, ada_b, ffn_w_gate, ffn_w_up, ffn_w_down,
           router_w, moe_w_gate, moe_w_up, moe_w_down, final_g):
    b, s, d = x.shape
    depth = w_in.shape[0]
    cos_t, sin_t = _rope_tables(positions)
    mods = _adaln_mod(c, ada_w, ada_b).reshape(depth, b, 6, d)
    for l in range(depth):
        mod = mods[l]
        wab = jnp.concatenate([_block_diag(lru_w_a[l]), _block_diag(lru_w_x[l])], axis=1).astype(BF16)
        bab = jnp.concatenate([lru_b_a[l], lru_b_x[l]]).reshape(1, -1)
        sp = jax.nn.softplus(-lru_lambda[l]).reshape(1, -1)
        q, kaug, vaug, o_lru, u = _inproj(x, mod, norm1_g[l], w_in[l].astype(BF16), cos_t, sin_t,
                                          lru_conv_w[l], lru_conv_b[l].reshape(1, -1), wab, bab, sp)
        o_att = _attention(q, kaug, vaug)
        wb, a1, wc = _s5_tables(s5_lambda_re[l], s5_lambda_im[l], s5_log_dt[l], s5_b_re[l], s5_b_im[l],
                                s5_c_re[l], s5_c_im[l])
        o_s5 = _s5(u, wb, a1, wc, s5_d[l].reshape(1, -1), s5_glu_w[l].astype(BF16), s5_glu_b[l].reshape(1, -1))
        fg = final_g if l == depth - 1 else None
        e = l // 2
        if l % 2 == 0:
            x = _outproj_ffn(o_att, o_lru, o_s5, x, mod, mix_gain[l], w_out[l].astype(BF16), norm2_g[l],
                             ffn_w_gate[e].astype(BF16), ffn_w_up[e].astype(BF16), ffn_w_down[e].astype(BF16), fg)
        else:
            x, h, ti, tw = _outproj_moe(o_att, o_lru, o_s5, x, mod, mix_gain[l], w_out[l].astype(BF16),
                                        norm2_g[l], router_w[e])
            dest, tile_expert, n_active, zero_tiles = _moe_plan(ti.reshape(b * s, 2))
            xs = _moe_dispatch(h.reshape(b * s, d), dest, zero_tiles, tile_expert.shape[0] * MOE_TILE)
            ys = _moe_experts(xs, tile_expert, n_active, moe_w_gate[e].astype(BF16), moe_w_up[e].astype(BF16),
                              moe_w_down[e].astype(BF16))
            x = _moe_combine(x, ys, dest, tw.reshape(b * s, 2), mod, fg)
    return x
```

```python
import functools
import math

import jax
import jax.numpy as jnp
from jax import lax
from jax.experimental import pallas as pl
from jax.experimental.pallas import tpu as pltpu

F32 = jnp.float32
BF16 = jnp.bfloat16

HEAD_DIM = 64
ATT_WIDTH = 384
LRU_WIDTH = 384
LRU_BLOCKS = 6
S5_WIDTH = 256
S5_GROUPS = 16
S5_GROUP_CH = 16
S5_STATE = 64
S5_NS = S5_GROUPS * S5_STATE
LRU_C = 8.0
MOBA_BLOCK = 256
MOBA_TOPK = 3
N_EXPERTS = 8
EPS = 1e-6
NEG = -1e30
LOG2E = math.log2(math.e)
LANES = 128
SUBLANES = 8
VMEM_LIMIT = 48 * 1024 * 1024
ATTN_VMEM_LIMIT = 56 * 1024 * 1024

TOK_TILE = 512
MOE_TILE = 512


def _params(*sem):
    return pltpu.CompilerParams(dimension_semantics=sem, vmem_limit_bytes=VMEM_LIMIT)


def _dot(a, b):
    return jnp.dot(a, b, preferred_element_type=F32)


def _dot_t(a, b):
    return lax.dot_general(a, b, (((1,), (1,)), ((), ())), preferred_element_type=F32)


def _sigmoid(x):
    return 1.0 / (1.0 + jnp.exp(-x))


def _gelu(x):
    c = math.sqrt(2.0 / math.pi)
    return 0.5 * x * (1.0 + jnp.tanh(c * (x + 0.044715 * (x * x * x))))


def _rms(x, g):
    ms = jnp.mean(x * x, axis=-1, keepdims=True)
    return (x * lax.rsqrt(ms + EPS)) * g


def _mod_kernel(c_ref, w_ref, b_ref, o_ref):
    c = c_ref[...]
    cond = (c * _sigmoid(c)).astype(BF16)
    o_ref[...] = _dot(cond, w_ref[...].astype(BF16)) + b_ref[...]


def _adaln_mod(c, ada_w, ada_b):
    n_layers, d, n = ada_w.shape
    b = c.shape[0]
    nc = n // 4
    return pl.pallas_call(
        _mod_kernel,
        grid=(n_layers, n // nc),
        in_specs=[pl.BlockSpec((b, d), lambda l, j: (0, 0)),
                  pl.BlockSpec((None, d, nc), lambda l, j: (l, 0, j)),
                  pl.BlockSpec((None, 1, nc), lambda l, j: (l, 0, j))],
        out_specs=pl.BlockSpec((None, b, nc), lambda l, j: (l, 0, j)),
        out_shape=jax.ShapeDtypeStruct((n_layers, b, n), F32),
        compiler_params=_params("arbitrary", "arbitrary"),
        name="adaln_mod",
    )(c, ada_w, ada_b.reshape(n_layers, 1, n))


def _inproj_kernel(x_ref, mod_ref, g_ref, wl_ref, w_ref, cos_ref, sin_ref, cw_ref, cb_ref, wab_ref, bab_ref,
                   sp_ref, q_ref, k_ref, v_ref, ol_ref, u_ref, hist_ref, hcar_ref):
    mod = mod_ref[...]
    h = (_rms(x_ref[...], g_ref[...]) * (1.0 + mod[1:2]) + mod[0:1]).astype(BF16)
    proj_l = _dot(h, wl_ref[...])

    @pl.when(pl.program_id(1) == 0)
    def _():
        hist_ref[0:8, :] = jnp.zeros((8, LRU_WIDTH), F32)
        hcar_ref[...] = jnp.zeros_like(hcar_ref)

    _lru_tile(proj_l[:, :LRU_WIDTH], proj_l[:, LRU_WIDTH:], cw_ref, cb_ref, wab_ref, bab_ref, sp_ref,
              ol_ref, hist_ref, hcar_ref)
    proj = _dot(h, w_ref[...])
    cos = cos_ref[...]
    sin = sin_ref[...]
    lane = lax.broadcasted_iota(jnp.int32, cos.shape, 1)
    first_half = (lane % HEAD_DIM) < (HEAD_DIM // 2)

    def rope(t):
        outs = []
        for i in range(ATT_WIDTH // LANES):
            xi = t[:, LANES * i:LANES * (i + 1)]
            partner = jnp.where(first_half,
                                pltpu.roll(xi, LANES - HEAD_DIM // 2, 1),
                                pltpu.roll(xi, HEAD_DIM // 2, 1))
            outs.append(xi * cos + partner * sin)
        return jnp.concatenate(outs, axis=1)

    a = ATT_WIDTH
    q_ref[...] = (rope(proj[:, 0:a]) * (HEAD_DIM ** -0.5 * LOG2E)).astype(BF16)
    k = rope(proj[:, a:2 * a]).astype(BF16)
    v = proj[:, 2 * a:3 * a].astype(BF16)
    tt = k.shape[0]
    blk_id = (pl.program_id(1) * tt + lax.broadcasted_iota(jnp.int32, (tt, LANES), 0)) // MOBA_BLOCK
    onehot = jnp.where(blk_id == lane, 1.0, 0.0).astype(BF16)
    ones = jnp.ones((tt, LANES), BF16)
    for i in range(a // LANES):
        k_ref[:, 2 * LANES * i:2 * LANES * i + LANES] = k[:, LANES * i:LANES * (i + 1)]
        k_ref[:, 2 * LANES * i + LANES:2 * LANES * (i + 1)] = onehot
        v_ref[:, 2 * LANES * i:2 * LANES * i + LANES] = v[:, LANES * i:LANES * (i + 1)]
        v_ref[:, 2 * LANES * i + LANES:2 * LANES * (i + 1)] = ones
    u_ref[...] = proj[:, 3 * a:]


def _inproj(x, mod, g, w_in_bf16, cos_t, sin_t, conv_w, conv_b, wab_bf16, bab, sp):
    b, s, d = x.shape
    tt = min(TOK_TILE, s)
    a = 3 * ATT_WIDTH
    w_lru = w_in_bf16[:, a:a + 2 * LRU_WIDTH]
    w_rest = jnp.concatenate([w_in_bf16[:, :a], w_in_bf16[:, a + 2 * LRU_WIDTH:]], axis=1)
    tok = lambda w: pl.BlockSpec((None, tt, w), lambda i, j: (i, j, 0))
    full = lambda arr: pl.BlockSpec(arr.shape, lambda i, j: (0,) * arr.ndim)
    g2 = g.reshape(1, d)
    outs = [jax.ShapeDtypeStruct((b, s, ATT_WIDTH), BF16)] + [jax.ShapeDtypeStruct((b, s, 2 * ATT_WIDTH), BF16)] * 2 + [
        jax.ShapeDtypeStruct((b, s, LRU_WIDTH), F32),
        jax.ShapeDtypeStruct((b, s, S5_WIDTH), F32)]
    return pl.pallas_call(
        _inproj_kernel,
        grid=(b, s // tt),
        in_specs=[tok(d), pl.BlockSpec((None, 6, d), lambda i, j: (i, 0, 0)), full(g2),
                  full(w_lru), full(w_rest), tok(LANES), tok(LANES),
                  full(conv_w), full(conv_b), full(wab_bf16), full(bab), full(sp)],
        out_specs=[tok(ATT_WIDTH), tok(2 * ATT_WIDTH), tok(2 * ATT_WIDTH), tok(LRU_WIDTH), tok(S5_WIDTH)],
        out_shape=outs,
        scratch_shapes=[pltpu.VMEM((8 + tt, LRU_WIDTH), F32), pltpu.VMEM((1, LRU_WIDTH), F32)],
        compiler_params=_params("arbitrary", "arbitrary"),
        name="inproj_lru",
    )(x, mod, g2, w_lru, w_rest, cos_t, sin_t, conv_w, conv_b, wab_bf16, bab, sp)


def _attn_kernel(q_ref, k_ref, v_ref, o_ref, kmean_ref, s_ref, mrun_ref, acc_ref, *, n_blk, n_pair):
    j = pl.program_id(1)
    blk = MOBA_BLOCK
    rows = 2 * blk
    kw = 2 * LANES

    @pl.when(j == 0)
    def _():
        for n in range(n_blk):
            for p in range(n_pair):
                kb = k_ref[n * blk:(n + 1) * blk, kw * p:kw * p + LANES]
                kmean_ref[n:n + 1, LANES * p:LANES * (p + 1)] = jnp.mean(kb.astype(F32), axis=0, keepdims=True)

    lane = lax.broadcasted_iota(jnp.int32, (blk, LANES), 1)
    bid = lax.broadcasted_iota(jnp.int32, (n_blk, rows), 0)
    valid = bid < j
    place = jnp.where(lax.broadcasted_iota(jnp.int32, (n_blk, LANES), 0)
                      == lax.broadcasted_iota(jnp.int32, (n_blk, LANES), 1), 1.0, 0.0).astype(BF16)
    own = pl.multiple_of(j * blk, blk)
    qpos = lax.broadcasted_iota(jnp.int32, (rows, blk), 0) % blk
    kpos = lax.broadcasted_iota(jnp.int32, (rows, blk), 1)
    causal = kpos <= qpos

    q_aug = []
    s_own = []
    for p in range(n_pair):
        q = q_ref[:, LANES * p:LANES * (p + 1)]
        zero = jnp.zeros_like(q)
        q2 = jnp.concatenate([jnp.where(lane < HEAD_DIM, q, zero),
                              jnp.where(lane >= HEAD_DIM, q, zero)], axis=0)
        kmean = kmean_ref[:, LANES * p:LANES * (p + 1)]
        km_hi = kmean.astype(BF16)
        km_lo = (kmean - km_hi.astype(F32)).astype(BF16)
        gw = jnp.where(valid, _dot_t(km_hi, q2) + _dot_t(km_lo, q2), NEG)
        sel = jnp.zeros(gw.shape, dtype=jnp.bool_)
        for _ in range(MOBA_TOPK):
            mx = jnp.max(gw, axis=0, keepdims=True)
            idx = jnp.min(jnp.where(gw == mx, bid, n_blk), axis=0, keepdims=True)
            pick = bid == idx
            sel = jnp.logical_or(sel, pick)
            gw = jnp.where(pick, -jnp.inf, gw)
        bias_t = jnp.where(jnp.logical_and(sel, valid), 0.0, NEG).astype(BF16)
        bias = lax.dot_general(bias_t, place, (((0,), (0,)), ((), ())), preferred_element_type=F32)
        q_aug.append(jnp.concatenate([q2, bias.astype(BF16)], axis=1))
        so = jnp.where(causal, _dot_t(q2, k_ref[pl.ds(own, blk), kw * p:kw * p + LANES]), NEG)
        s_own.append(so)
        mrun_ref[p] = jnp.maximum(so[:, :LANES], so[:, LANES:])

    n_two = (j + 1) // 2

    def walk(body):
        def pair_of(i, carry):
            body(2 * i)
            body(2 * i + 1)
            return carry
        lax.fori_loop(0, n_two // 2, pair_of, 0)

        @pl.when(n_two % 2 == 1)
        def _():
            body(n_two - 1)

    def scores(i):
        start = pl.multiple_of(i * (2 * blk), 2 * blk)
        for p in range(n_pair):
            sn = _dot_t(q_aug[p], k_ref[pl.ds(start, 2 * blk), kw * p:kw * (p + 1)])
            s_ref[p, i] = sn
            top = jnp.maximum(jnp.maximum(sn[:, :LANES], sn[:, LANES:2 * LANES]),
                              jnp.maximum(sn[:, 2 * LANES:3 * LANES], sn[:, 3 * LANES:]))
            mrun_ref[p] = jnp.maximum(mrun_ref[p], top)

    walk(scores)

    m = []
    for p in range(n_pair):
        mp = jnp.max(mrun_ref[p], axis=1, keepdims=True)
        m.append(mp)
        acc_ref[p] = _dot(jnp.exp2(s_own[p] - mp).astype(BF16), v_ref[pl.ds(own, blk), kw * p:kw * (p + 1)])

    def values(i):
        start = pl.multiple_of(i * (2 * blk), 2 * blk)
        for p in range(n_pair):
            pn = jnp.exp2(s_ref[p, i] - m[p]).astype(BF16)
            acc_ref[p] += _dot(pn, v_ref[pl.ds(start, 2 * blk), kw * p:kw * (p + 1)])

    walk(values)
    for p in range(n_pair):
        acc = acc_ref[p]
        out = acc[:, :LANES] / acc[:, LANES:]
        o_ref[:, LANES * p:LANES * (p + 1)] = jnp.where(lane < HEAD_DIM, out[:blk], out[blk:])


def _attention(q, kaug, vaug):
    b, s, w = q.shape
    n_blk = s // MOBA_BLOCK
    n_pair = w // LANES
    rows = 2 * MOBA_BLOCK
    once = pl.Buffered(1)
    return pl.pallas_call(
        functools.partial(_attn_kernel, n_blk=n_blk, n_pair=n_pair),
        grid=(b, n_blk),
        in_specs=[pl.BlockSpec((None, MOBA_BLOCK, w), lambda i, j: (i, j, 0)),
                  pl.BlockSpec((None, s, 2 * w), lambda i, j: (i, 0, 0), pipeline_mode=once),
                  pl.BlockSpec((None, s, 2 * w), lambda i, j: (i, 0, 0), pipeline_mode=once)],
        out_specs=pl.BlockSpec((None, MOBA_BLOCK, w), lambda i, j: (i, j, 0)),
        out_shape=jax.ShapeDtypeStruct((b, s, w), F32),
        scratch_shapes=[pltpu.VMEM((n_blk, w), F32),
                        pltpu.VMEM((n_pair, n_blk // 2, rows, 2 * MOBA_BLOCK), F32),
                        pltpu.VMEM((n_pair, rows, LANES), F32),
                        pltpu.VMEM((n_pair, rows, 2 * LANES), F32)],
        compiler_params=pltpu.CompilerParams(dimension_semantics=("arbitrary", "arbitrary"),
                                             vmem_limit_bytes=ATTN_VMEM_LIMIT),
        name="moba_attention",
    )(q, kaug, vaug)


def _lru_tile(xr, gate, cw_ref, cb_ref, wab_ref, bab_ref, sp_ref, o_ref, hist_ref, h_ref):
    lt = xr.shape[0]
    w = LRU_WIDTH
    hist_ref[8:8 + lt, :] = xr
    cw = cw_ref[...]
    n_tap = cw.shape[0]
    xc = cb_ref[...] + cw[n_tap - 1:n_tap] * xr
    for back in range(1, n_tap):
        xc = xc + cw[n_tap - 1 - back:n_tap - back] * hist_ref[8 - back:8 - back + lt, :]
    hist_ref[0:8, :] = xr[lt - 8:lt]

    ri = _sigmoid(_dot(xc.astype(BF16), wab_ref[...]) + bab_ref[...])
    r = ri[:, :w]
    i = ri[:, w:]
    log_a = (-LRU_C) * r * sp_ref[...]
    a = jnp.exp(log_a)
    expm1 = jnp.tanh(log_a) * (a * a + 1.0)
    u = jnp.sqrt(jnp.maximum(-expm1, 0.0)) * (i * xc)

    n_grp = lt // SUBLANES
    u = u.reshape(n_grp, SUBLANES, w)
    a = a.reshape(n_grp, SUBLANES, w)
    row = lax.broadcasted_iota(jnp.int32, (SUBLANES, w), 0)
    for k in range(3):
        d = 1 << k
        inside = row >= d
        u = u + jnp.where(inside, a, 0.0) * pltpu.roll(u, d, 1)
        a = a * jnp.where(inside, pltpu.roll(a, d, 1), 1.0)
    gg = _gelu(gate)
    carry = h_ref[...]
    for g in range(n_grp):
        rows = slice(g * SUBLANES, (g + 1) * SUBLANES)
        hg = u[g] + a[g] * carry
        o_ref[rows, :] = hg * gg[rows]
        carry = hg[SUBLANES - 1:SUBLANES]
    h_ref[...] = carry


S5_STEPS = 64


def _s5_kernel(u_ref, wb_ref, a_ref, wc_ref, d_ref, gw_ref, gb_ref, o_ref, xs_ref, bu_ref, st_ref):
    nb, lt, w = u_ref.shape
    ns = S5_NS
    rows = nb * lt

    @pl.when(pl.program_id(0) == 0)
    def _():
        xs_ref[...] = jnp.zeros_like(xs_ref)

    u = u_ref[...].reshape(rows, w)
    r = lax.broadcasted_iota(jnp.int32, (rows, rows), 0)
    c = lax.broadcasted_iota(jnp.int32, (rows, rows), 1)
    to_tm = jnp.where((r % nb) * lt + r // nb == c, 1.0, 0.0).astype(BF16)
    to_bt = jnp.where((r % lt) * nb + r // lt == c, 1.0, 0.0).astype(BF16)
    u_tm = _dot(to_tm, u.astype(BF16)).astype(BF16)
    bu_ref[...] = _dot(u_tm, wb_ref[...]).reshape(lt, nb, 2 * ns)
    ar = jnp.broadcast_to(a_ref[0:1, :], (nb, ns))
    ai = jnp.broadcast_to(a_ref[1:2, :], (nb, ns))

    def step(t, carry):
        re, im = carry
        b = bu_ref[t]
        re, im = ar * re - ai * im + b[:, :ns], ar * im + ai * re + b[:, ns:]
        st_ref[t] = jnp.concatenate([re, im], axis=1)
        return re, im

    x0 = xs_ref[...]
    carry = (x0[:, :ns], x0[:, ns:])
    for t in range(lt):
        carry = step(t, carry)
    re, im = carry
    xs_ref[...] = jnp.concatenate([re, im], axis=1)
    y_tm = _dot(st_ref[...].reshape(rows, 2 * ns).astype(BF16), wc_ref[...])
    y_hi = y_tm.astype(BF16)
    y_lo = (y_tm - y_hi.astype(F32)).astype(BF16)
    y_bt = _dot(to_bt, y_hi) + _dot(to_bt, y_lo)
    y = _gelu(y_bt + d_ref[...] * u)
    z = _sigmoid(_dot(y.astype(BF16), gw_ref[...]) + gb_ref[...])
    o_ref[...] = (y * z).reshape(nb, lt, w)


def _s5(u, wb_bf16, a1, wc_bf16, d_skip, glu_w_bf16, glu_b):
    b, s, w = u.shape
    lt = min(S5_STEPS, s)
    tok = pl.BlockSpec((b, lt, w), lambda i: (0, i, 0))
    full = lambda a: pl.BlockSpec(a.shape, lambda i: (0,) * a.ndim)
    return pl.pallas_call(
        _s5_kernel,
        grid=(s // lt,),
        in_specs=[tok, full(wb_bf16), full(a1), full(wc_bf16), full(d_skip), full(glu_w_bf16), full(glu_b)],
        out_specs=tok,
        out_shape=jax.ShapeDtypeStruct((b, s, w), F32),
        scratch_shapes=[pltpu.VMEM((b, 2 * S5_NS), F32), pltpu.VMEM((lt, b, 2 * S5_NS), F32),
                        pltpu.VMEM((lt, b, 2 * S5_NS), F32)],
        compiler_params=_params("arbitrary"),
        name="s5",
    )(u, wb_bf16, a1, wc_bf16, d_skip, glu_w_bf16, glu_b)


def _s5_tables(lam_re, lam_im, log_dt, b_re, b_im, c_re, c_im):
    g, p, h = b_re.shape
    dt = jnp.exp(log_dt)[:, None]
    mag = jnp.exp(lam_re * dt)
    ab_re, ab_im = mag * jnp.cos(lam_im * dt), mag * jnp.sin(lam_im * dt)
    den = lam_re * lam_re + lam_im * lam_im
    nr, ni = ab_re - 1.0, ab_im
    co_re = (nr * lam_re + ni * lam_im) / den
    co_im = (ni * lam_re - nr * lam_im) / den
    bb_re = co_re[..., None] * b_re - co_im[..., None] * b_im
    bb_im = co_re[..., None] * b_im + co_im[..., None] * b_re
    eye = jnp.eye(g, dtype=F32)
    wb = jnp.concatenate([jnp.einsum('gph,gk->ghkp', bb_re, eye).reshape(g * h, g * p),
                          jnp.einsum('gph,gk->ghkp', bb_im, eye).reshape(g * h, g * p)], axis=1)
    wc = jnp.concatenate([jnp.einsum('ghp,gk->kpgh', c_re, eye).reshape(g * p, g * h),
                          -jnp.einsum('ghp,gk->kpgh', c_im, eye).reshape(g * p, g * h)], axis=0)

    a1 = jnp.stack([ab_re.reshape(g * p), ab_im.reshape(g * p)])
    return wb.astype(BF16), a1, wc.astype(BF16)


def _mix_residual(oa_ref, ol_ref, os_ref, x_ref, mod, mg_ref, w_ref, g2_ref):
    mg = mg_ref[...]
    a, l = ATT_WIDTH, LRU_WIDTH
    o = (_dot(_rms(oa_ref[...], mg[:, :a]).astype(BF16), w_ref[:a, :])
         + _dot(_rms(ol_ref[...], mg[:, a:a + l]).astype(BF16), w_ref[a:a + l, :])
         + _dot(_rms(os_ref[...], mg[:, a + l:]).astype(BF16), w_ref[a + l:, :]))
    x = x_ref[...] + mod[2:3] * o
    h = _rms(x, g2_ref[...]) * (1.0 + mod[4:5]) + mod[3:4]
    return x, h


def _outproj_ffn_kernel(oa_ref, ol_ref, os_ref, x_ref, mod_ref, mg_ref, w_ref, g2_ref,
                        wg_ref, wu_ref, wd_ref, fg_ref, o_ref, *, final):
    mod = mod_ref[...]
    x, h = _mix_residual(oa_ref, ol_ref, os_ref, x_ref, mod, mg_ref, w_ref, g2_ref)
    hb = h.astype(BF16)
    g = _dot(hb, wg_ref[...])
    u = _dot(hb, wu_ref[...])
    f = _dot(((g * _sigmoid(g)) * u).astype(BF16), wd_ref[...])
    x = x + mod[5:6] * f
    o_ref[...] = _rms(x, fg_ref[...]) if final else x


def _outproj_moe_kernel(oa_ref, ol_ref, os_ref, x_ref, mod_ref, mg_ref, w_ref, g2_ref, rwt_ref,
                        x_out, h_out, ti_out, tw_out):
    x, h = _mix_residual(oa_ref, ol_ref, os_ref, x_ref, mod_ref[...], mg_ref, w_ref, g2_ref)
    x_out[...] = x
    h_out[...] = h
    hb = h.astype(BF16)
    h_lo = (h - hb.astype(F32)).astype(BF16)
    rwt = rwt_ref[...]
    rw_hi = rwt.astype(BF16)
    rw_lo = (rwt - rw_hi.astype(F32)).astype(BF16)
    logits = _dot_t(rw_hi, hb) + _dot_t(rw_lo, hb) + _dot_t(rw_hi, h_lo)
    eid = lax.broadcasted_iota(jnp.int32, logits.shape, 0)
    m1 = jnp.max(logits, axis=0, keepdims=True)
    i1 = jnp.min(jnp.where(logits == m1, eid, N_EXPERTS), axis=0, keepdims=True)
    rest_l = jnp.where(eid == i1, -jnp.inf, logits)
    m2 = jnp.max(rest_l, axis=0, keepdims=True)
    i2 = jnp.min(jnp.where(rest_l == m2, eid, N_EXPERTS), axis=0, keepdims=True)
    e2 = jnp.exp(m2 - m1)
    den = 1.0 + e2
    ti_out[...] = jnp.concatenate([i1, i2], axis=0)
    tw_out[...] = jnp.concatenate([1.0 / den, e2 / den], axis=0)


def _outproj_specs(x, tt):
    b, s, d = x.shape
    tok = lambda w: pl.BlockSpec((None, tt, w), lambda i, j: (i, j, 0))
    once = lambda r, c: pl.BlockSpec((r, c), lambda i, j: (0, 0), pipeline_mode=pl.Buffered(1))
    specs = [tok(ATT_WIDTH), tok(LRU_WIDTH), tok(S5_WIDTH), tok(d),
             pl.BlockSpec((None, 6, d), lambda i, j: (i, 0, 0)), once(1, d), once(d, d), once(1, d)]
    return tok, once, specs


def _outproj_ffn(o_att, o_lru, o_s5, x, mod, mix_gain, w_out_bf16, g2, wg, wu, wd, final_g):
    b, s, d = x.shape
    tt = min(TOK_TILE, s)
    dff = wg.shape[1]
    tok, once, specs = _outproj_specs(x, tt)
    fg = jnp.ones((d,), F32) if final_g is None else final_g
    return pl.pallas_call(
        functools.partial(_outproj_ffn_kernel, final=final_g is not None),
        grid=(b, s // tt),
        in_specs=specs + [once(d, dff), once(d, dff), once(dff, d), once(1, d)],
        out_specs=tok(d),
        out_shape=jax.ShapeDtypeStruct((b, s, d), F32),
        compiler_params=pltpu.CompilerParams(dimension_semantics=("arbitrary", "arbitrary"),
                                             vmem_limit_bytes=ATTN_VMEM_LIMIT),
        name="outproj_ffn",
    )(o_att, o_lru, o_s5, x, mod, mix_gain.reshape(1, d), w_out_bf16, g2.reshape(1, d), wg, wu, wd,
      fg.reshape(1, d))


def _outproj_moe(o_att, o_lru, o_s5, x, mod, mix_gain, w_out_bf16, g2, router_w):
    b, s, d = x.shape
    tt = min(TOK_TILE, s)
    tok, once, specs = _outproj_specs(x, tt)
    top = pl.BlockSpec((None, 2, tt), lambda i, j: (i, 0, j))
    x_new, h, ti, tw = pl.pallas_call(
        _outproj_moe_kernel,
        grid=(b, s // tt),
        in_specs=specs + [once(N_EXPERTS, d)],
        out_specs=[tok(d), tok(d), top, top],
        out_shape=[jax.ShapeDtypeStruct((b, s, d), F32), jax.ShapeDtypeStruct((b, s, d), F32),
                   jax.ShapeDtypeStruct((b, 2, s), jnp.int32), jax.ShapeDtypeStruct((b, 2, s), F32)],
        compiler_params=_params("arbitrary", "arbitrary"),
        name="outproj_moe",
    )(o_att, o_lru, o_s5, x, mod, mix_gain.reshape(1, d), w_out_bf16, g2.reshape(1, d), router_w.T)
    return x_new, h, jnp.swapaxes(ti, 1, 2), jnp.swapaxes(tw, 1, 2)


def _rope_tables(positions):
    half = HEAD_DIM // 2
    inv = 10000.0 ** (-jnp.arange(0, HEAD_DIM, 2, dtype=F32) / HEAD_DIM)
    ang = positions.astype(F32)[..., None] * inv
    cos, sin = jnp.cos(ang), jnp.sin(ang)
    reps = LANES // HEAD_DIM
    cos_t = jnp.tile(jnp.concatenate([cos, cos], axis=-1), (1, 1, reps))
    sin_t = jnp.tile(jnp.concatenate([-sin, sin], axis=-1), (1, 1, reps))
    return cos_t, sin_t


def _block_diag(w):
    h, i, j = w.shape
    eye = jnp.eye(h, dtype=w.dtype)
    return jnp.einsum('hij,hk->hikj', w, eye).reshape(h * i, h * j)


def _moe_plan(ti):
    t = ti.shape[0]
    tm = MOE_TILE
    n_tiles = (2 * t) // tm + N_EXPERTS
    onehot = jnp.any(ti[:, :, None] == jnp.arange(N_EXPERTS, dtype=jnp.int32), axis=1)
    pos = jnp.cumsum(onehot.astype(jnp.int32), axis=0) - 1
    counts = pos[-1] + 1
    padded = ((counts + tm - 1) // tm) * tm
    ends = jnp.cumsum(padded)
    starts = ends - padded
    dest = jnp.take_along_axis(starts[None, :] + pos, ti, axis=1)
    tile_start = jnp.arange(n_tiles, dtype=jnp.int32) * tm
    tile_expert = jnp.minimum(jnp.sum((tile_start[:, None] >= ends[None, :]).astype(jnp.int32), axis=1),
                              N_EXPERTS - 1)
    n_active = ends[-1] // tm
    pad_tile = jnp.where(padded > counts, ends // tm - 1, -1)
    tail_tile = n_tiles - 1 - jnp.arange(N_EXPERTS, dtype=jnp.int32)
    zero_tiles = jnp.concatenate([pad_tile, jnp.where(tail_tile >= n_active, tail_tile, -1)]).astype(jnp.int32)
    return dest.astype(jnp.int32), tile_expert, n_active.astype(jnp.int32).reshape(1), zero_tiles


def _dispatch_kernel(zt_ref, dest_ref, h_hbm, xs_hbm, hbuf, zbuf, lsem, ssem, zsem):
    i = pl.program_id(0)
    n = pl.num_programs(0)
    tt = hbuf.shape[1]
    tm = zbuf.shape[0]

    def load(step, slot):
        start = pl.multiple_of(step * tt, tt)
        return pltpu.make_async_copy(h_hbm.at[pl.ds(start, tt), :], hbuf.at[slot], lsem.at[slot])

    def wait_rows(slot):
        for _ in range(2):
            pltpu.make_async_copy(hbuf.at[slot], xs_hbm.at[pl.ds(0, tt), :], ssem.at[slot]).wait()

    @pl.when(i == 0)
    def _():
        load(0, 0).start()
        zbuf[...] = jnp.zeros_like(zbuf)

        def clear(k):
            start = pl.multiple_of(zt_ref[k] * tm, tm)
            return pltpu.make_async_copy(zbuf, xs_hbm.at[pl.ds(start, tm), :], zsem)

        for k in range(zt_ref.shape[0]):
            @pl.when(zt_ref[k] >= 0)
            def _():
                clear(k).start()
        for k in range(zt_ref.shape[0]):
            @pl.when(zt_ref[k] >= 0)
            def _():
                clear(k).wait()

    for slot in range(2):
        @pl.when(i % 2 == slot)
        def _():
            @pl.when(i >= 1)
            def _():
                wait_rows(1 - slot)

            @pl.when(i + 1 < n)
            def _():
                load(i + 1, 1 - slot).start()

            load(i, slot).wait()
            for r in range(tt):
                for choice in range(2):
                    t = dest_ref[0, 0, 2 * r + choice]
                    pltpu.make_async_copy(hbuf.at[slot, pl.ds(r, 1), :], xs_hbm.at[pl.ds(t, 1), :],
                                          ssem.at[slot]).start(priority=choice)

            @pl.when(i == n - 1)
            def _():
                wait_rows(slot)


def _moe_dispatch(h2d, dest, zero_tiles, n_pad):
    t, d = h2d.shape
    tt = min(TOK_TILE, t)
    grid_spec = pltpu.PrefetchScalarGridSpec(
        num_scalar_prefetch=1,
        grid=(t // tt,),
        in_specs=[pl.BlockSpec((1, 1, 2 * tt), lambda i, zt: (i, 0, 0), memory_space=pltpu.SMEM),
                  pl.BlockSpec(memory_space=pl.ANY)],
        out_specs=pl.BlockSpec(memory_space=pl.ANY),
        scratch_shapes=[pltpu.VMEM((2, tt, d), F32), pltpu.VMEM((MOE_TILE, d), F32),
                        pltpu.SemaphoreType.DMA((2,)), pltpu.SemaphoreType.DMA((2,)),
                        pltpu.SemaphoreType.DMA(())])
    return pl.pallas_call(
        _dispatch_kernel,
        grid_spec=grid_spec,
        out_shape=jax.ShapeDtypeStruct((n_pad, d), F32),
        compiler_params=_params("arbitrary"),
        name="moe_dispatch",
    )(zero_tiles, dest.reshape(t // tt, 1, 2 * tt), h2d)


def _experts_kernel(te_ref, na_ref, x_ref, wg_ref, wu_ref, wd_ref, y_ref):
    active = pl.program_id(0) < na_ref[0]

    @pl.when(active)
    def _():
        x = x_ref[...].astype(BF16)
        g = _dot(x, wg_ref[...])
        u = _dot(x, wu_ref[...])
        y_ref[...] = _dot(((g * _sigmoid(g)) * u).astype(BF16), wd_ref[...])

    @pl.when(jnp.logical_not(active))
    def _():
        y_ref[...] = jnp.zeros_like(y_ref)


def _moe_experts(xs, tile_expert, n_active, wg, wu, wd):
    n_pad, d = xs.shape
    tm = MOE_TILE
    dff = wg.shape[2]
    once = pl.Buffered(1)
    grid_spec = pltpu.PrefetchScalarGridSpec(
        num_scalar_prefetch=2,
        grid=(n_pad // tm,),
        in_specs=[pl.BlockSpec((tm, d), lambda i, te, na: (i, 0)),
                  pl.BlockSpec((None, d, dff), lambda i, te, na: (te[i], 0, 0), pipeline_mode=once),
                  pl.BlockSpec((None, d, dff), lambda i, te, na: (te[i], 0, 0), pipeline_mode=once),
                  pl.BlockSpec((None, dff, d), lambda i, te, na: (te[i], 0, 0), pipeline_mode=once)],
        out_specs=pl.BlockSpec((tm, d), lambda i, te, na: (i, 0)))
    return pl.pallas_call(
        _experts_kernel,
        grid_spec=grid_spec,
        out_shape=jax.ShapeDtypeStruct((n_pad, d), F32),
        compiler_params=pltpu.CompilerParams(dimension_semantics=("arbitrary",),
                                             vmem_limit_bytes=ATTN_VMEM_LIMIT),
        name="moe_experts",
    )(tile_expert, n_active, xs, wg, wu, wd)


def _combine_kernel(dcur_ref, dnext_ref, ys_hbm, x_ref, tw_ref, mod_ref, g_ref, o_ref, buf, sem, *, final):
    i = pl.program_id(0)
    n = pl.num_programs(0)
    tt = x_ref.shape[0]

    def gather(d_ref, slot):
        for r in range(tt):
            for choice in range(2):
                t = d_ref[0, 0, 2 * r + choice]
                pltpu.make_async_copy(ys_hbm.at[pl.ds(t, 1), :], buf.at[slot, choice, pl.ds(r, 1), :],
                                      sem.at[slot]).start(priority=choice)

    @pl.when(i == 0)
    def _():
        gather(dcur_ref, 0)

    slot = i % 2
    for nxt in range(2):
        @pl.when(jnp.logical_and(i + 1 < n, slot == 1 - nxt))
        def _():
            gather(dnext_ref, nxt)

    for choice in range(2):
        pltpu.make_async_copy(ys_hbm.at[pl.ds(0, tt), :], buf.at[slot, choice], sem.at[slot]).wait()
    tw = tw_ref[...]
    y = tw[:, 0:1] * buf[slot, 0] + tw[:, 1:2] * buf[slot, 1]
    x = x_ref[...] + mod_ref[...][5:6] * y
    o_ref[...] = _rms(x, g_ref[...]) if final else x


def _moe_combine(x, ys, dest, tw, mod, final_g):
    b, s, d = x.shape
    t = b * s
    tt = min(TOK_TILE, s)
    n = t // tt
    per_b = s // tt
    dest3 = dest.reshape(n, 1, 2 * tt)
    return pl.pallas_call(
        functools.partial(_combine_kernel, final=final_g is not None),
        grid=(n,),
        in_specs=[pl.BlockSpec((1, 1, 2 * tt), lambda i: (i, 0, 0), memory_space=pltpu.SMEM),
                  pl.BlockSpec((1, 1, 2 * tt), lambda i: (jnp.minimum(i + 1, n - 1), 0, 0), memory_space=pltpu.SMEM),
                  pl.BlockSpec(memory_space=pl.ANY),
                  pl.BlockSpec((tt, d), lambda i: (i, 0)),
                  pl.BlockSpec((tt, 2), lambda i: (i, 0)),
                  pl.BlockSpec((None, 6, d), lambda i: (i // per_b, 0, 0)),
                  pl.BlockSpec((1, d), lambda i: (0, 0))],
        out_specs=pl.BlockSpec((tt, d), lambda i: (i, 0)),
        out_shape=jax.ShapeDtypeStruct((t, d), F32),
        scratch_shapes=[pltpu.VMEM((2, 2, tt, d), F32), pltpu.SemaphoreType.DMA((2,))],
        compiler_params=_params("arbitrary"),
        name="moe_combine",
    )(dest3, dest3, ys, x.reshape(t, d), tw, mod,
      (jnp.ones((d,), F32) if final_g is None else final_g).reshape(1, d)).reshape(b, s, d)


def kernel(x, c, positions, w_in, lru_conv_w, lru_conv_b, lru_w_a, lru_b_a, lru_w_x, lru_b_x, lru_lambda,
           s5_lambda_re, s5_lambda_im, s5_log_dt, s5_b_re, s5_b_im, s5_c_re, s5_c_im, s5_d, s5_glu_w,
           s5_glu_b, mix_gain, w_out, norm1_g, norm2_g, ada_w, ada_b, ffn_w_gate, ffn_w_up, ffn_w_down,
           router_w, moe_w_gate, moe_w_up, moe_w_down, final_g):
    b, s, d = x.shape
    depth = w_in.shape[0]
    cos_t, sin_t = _rope_tables(positions)
    mods = _adaln_mod(c, ada_w, ada_b).reshape(depth, b, 6, d)
    for l in range(depth):
        mod = mods[l]
        wab = jnp.concatenate([_block_diag(lru_w_a[l]), _block_diag(lru_w_x[l])], axis=1).astype(BF16)
        bab = jnp.concatenate([lru_b_a[l], lru_b_x[l]]).reshape(1, -1)
        sp = jax.nn.softplus(-lru_lambda[l]).reshape(1, -1)
        q, kaug, vaug, o_lru, u = _inproj(x, mod, norm1_g[l], w_in[l].astype(BF16), cos_t, sin_t,
                                          lru_conv_w[l], lru_conv_b[l].reshape(1, -1), wab, bab, sp)
        o_att = _attention(q, kaug, vaug)
        wb, a1, wc = _s5_tables(s5_lambda_re[l], s5_lambda_im[l], s5_log_dt[l], s5_b_re[l], s5_b_im[l],
                                s5_c_re[l], s5_c_im[l])
        o_s5 = _s5(u, wb, a1, wc, s5_d[l].reshape(1, -1), s5_glu_w[l].astype(BF16), s5_glu_b[l].reshape(1, -1))
        fg = final_g if l == depth - 1 else None
        e = l // 2
        if l % 2 == 0:
            x = _outproj_ffn(o_att, o_lru, o_s5, x, mod, mix_gain[l], w_out[l].astype(BF16), norm2_g[l],
                             ffn_w_gate[e].astype(BF16), ffn_w_up[e].astype(BF16), ffn_w_down[e].astype(BF16), fg)
        else:
            x, h, ti, tw = _outproj_moe(o_att, o_lru, o_s5, x, mod, mix_gain[l], w_out[l].astype(BF16),
                                        norm2_g[l], router_w[e])
            dest, tile_expert, n_active, zero_tiles = _moe_plan(ti.reshape(b * s, 2))
            xs = _moe_dispatch(h.reshape(b * s, d), dest, zero_tiles, tile_expert.shape[0] * MOE_TILE)
            ys = _moe_experts(xs, tile_expert, n_active, moe_w_gate[e].astype(BF16), moe_w_up[e].astype(BF16),
                              moe_w_down[e].astype(BF16))
            x = _moe_combine(x, ys, dest, tw.reshape(b * s, 2), mod, fg)
    return x
```

```python
import functools
import math

import jax
import jax.numpy as jnp
from jax import lax
from jax.experimental import pallas as pl
from jax.experimental.pallas import tpu as pltpu

F32 = jnp.float32
BF16 = jnp.bfloat16

HEAD_DIM = 64
ATT_WIDTH = 384
LRU_WIDTH = 384
LRU_BLOCKS = 6
S5_WIDTH = 256
S5_GROUPS = 16
S5_GROUP_CH = 16
S5_STATE = 64
S5_NS = S5_GROUPS * S5_STATE
LRU_C = 8.0
MOBA_BLOCK = 256
MOBA_TOPK = 3
N_EXPERTS = 8
EPS = 1e-6
NEG = -1e30
LOG2E = math.log2(math.e)
LANES = 128
SUBLANES = 8
VMEM_LIMIT = 48 * 1024 * 1024
ATTN_VMEM_LIMIT = 56 * 1024 * 1024

TOK_TILE = 512
MOE_TILE = 512


def _params(*sem):
    return pltpu.CompilerParams(dimension_semantics=sem, vmem_limit_bytes=VMEM_LIMIT)


def _dot(a, b):
    return jnp.dot(a, b, preferred_element_type=F32)


def _dot_t(a, b):
    return lax.dot_general(a, b, (((1,), (1,)), ((), ())), preferred_element_type=F32)


def _sigmoid(x):
    return 1.0 / (1.0 + jnp.exp(-x))


def _gelu(x):
    c = math.sqrt(2.0 / math.pi)
    return 0.5 * x * (1.0 + jnp.tanh(c * (x + 0.044715 * (x * x * x))))


def _rms(x, g):
    ms = jnp.mean(x * x, axis=-1, keepdims=True)
    return (x * lax.rsqrt(ms + EPS)) * g


def _mod_kernel(c_ref, w_ref, b_ref, o_ref):
    c = c_ref[...]
    cond = (c * _sigmoid(c)).astype(BF16)
    o_ref[...] = _dot(cond, w_ref[...].astype(BF16)) + b_ref[...]


def _adaln_mod(c, ada_w, ada_b):
    n_layers, d, n = ada_w.shape
    b = c.shape[0]
    nc = n // 4
    return pl.pallas_call(
        _mod_kernel,
        grid=(n_layers, n // nc),
        in_specs=[pl.BlockSpec((b, d), lambda l, j: (0, 0)),
                  pl.BlockSpec((None, d, nc), lambda l, j: (l, 0, j)),
                  pl.BlockSpec((None, 1, nc), lambda l, j: (l, 0, j))],
        out_specs=pl.BlockSpec((None, b, nc), lambda l, j: (l, 0, j)),
        out_shape=jax.ShapeDtypeStruct((n_layers, b, n), F32),
        compiler_params=_params("arbitrary", "arbitrary"),
        name="adaln_mod",
    )(c, ada_w, ada_b.reshape(n_layers, 1, n))


def _inproj_kernel(x_ref, mod_ref, g_ref, wl_ref, w_ref, cos_ref, sin_ref, cw_ref, cb_ref, wab_ref, bab_ref,
                   sp_ref, q_ref, k_ref, v_ref, ol_ref, u_ref, hist_ref, hcar_ref):
    mod = mod_ref[...]
    h = (_rms(x_ref[...], g_ref[...]) * (1.0 + mod[1:2]) + mod[0:1]).astype(BF16)
    proj_l = _dot(h, wl_ref[...])

    @pl.when(pl.program_id(1) == 0)
    def _():
        hist_ref[0:8, :] = jnp.zeros((8, LRU_WIDTH), F32)
        hcar_ref[...] = jnp.zeros_like(hcar_ref)

    _lru_tile(proj_l[:, :LRU_WIDTH], proj_l[:, LRU_WIDTH:], cw_ref, cb_ref, wab_ref, bab_ref, sp_ref,
              ol_ref, hist_ref, hcar_ref)
    proj = _dot(h, w_ref[...])
    cos = cos_ref[...]
    sin = sin_ref[...]
    lane = lax.broadcasted_iota(jnp.int32, cos.shape, 1)
    first_half = (lane % HEAD_DIM) < (HEAD_DIM // 2)

    def rope(t):
        outs = []
        for i in range(ATT_WIDTH // LANES):
            xi = t[:, LANES * i:LANES * (i + 1)]
            partner = jnp.where(first_half,
                                pltpu.roll(xi, LANES - HEAD_DIM // 2, 1),
                                pltpu.roll(xi, HEAD_DIM // 2, 1))
            outs.append(xi * cos + partner * sin)
        return jnp.concatenate(outs, axis=1)

    a = ATT_WIDTH
    q_ref[...] = (rope(proj[:, 0:a]) * (HEAD_DIM ** -0.5 * LOG2E)).astype(BF16)
    k = rope(proj[:, a:2 * a]).astype(BF16)
    v = proj[:, 2 * a:3 * a].astype(BF16)
    tt = k.shape[0]
    blk_id = (pl.program_id(1) * tt + lax.broadcasted_iota(jnp.int32, (tt, LANES), 0)) // MOBA_BLOCK
    onehot = jnp.where(blk_id == lane, 1.0, 0.0).astype(BF16)
    ones = jnp.ones((tt, LANES), BF16)
    for i in range(a // LANES):
        k_ref[:, 2 * LANES * i:2 * LANES * i + LANES] = k[:, LANES * i:LANES * (i + 1)]
        k_ref[:, 2 * LANES * i + LANES:2 * LANES * (i + 1)] = onehot
        v_ref[:, 2 * LANES * i:2 * LANES * i + LANES] = v[:, LANES * i:LANES * (i + 1)]
        v_ref[:, 2 * LANES * i + LANES:2 * LANES * (i + 1)] = ones
    u_ref[...] = proj[:, 3 * a:]


def _inproj(x, mod, g, w_in_bf16, cos_t, sin_t, conv_w, conv_b, wab_bf16, bab, sp):
    b, s, d = x.shape
    tt = min(TOK_TILE, s)
    a = 3 * ATT_WIDTH
    w_lru = w_in_bf16[:, a:a + 2 * LRU_WIDTH]
    w_rest = jnp.concatenate([w_in_bf16[:, :a], w_in_bf16[:, a + 2 * LRU_WIDTH:]], axis=1)
    tok = lambda w: pl.BlockSpec((None, tt, w), lambda i, j: (i, j, 0))
    full = lambda arr: pl.BlockSpec(arr.shape, lambda i, j: (0,) * arr.ndim)
    g2 = g.reshape(1, d)
    outs = [jax.ShapeDtypeStruct((b, s, ATT_WIDTH), BF16)] + [jax.ShapeDtypeStruct((b, s, 2 * ATT_WIDTH), BF16)] * 2 + [
        jax.ShapeDtypeStruct((b, s, LRU_WIDTH), F32),
        jax.ShapeDtypeStruct((b, s, S5_WIDTH), F32)]
    return pl.pallas_call(
        _inproj_kernel,
        grid=(b, s // tt),
        in_specs=[tok(d), pl.BlockSpec((None, 6, d), lambda i, j: (i, 0, 0)), full(g2),
                  full(w_lru), full(w_rest), tok(LANES), tok(LANES),
                  full(conv_w), full(conv_b), full(wab_bf16), full(bab), full(sp)],
        out_specs=[tok(ATT_WIDTH), tok(2 * ATT_WIDTH), tok(2 * ATT_WIDTH), tok(LRU_WIDTH), tok(S5_WIDTH)],
        out_shape=outs,
        scratch_shapes=[pltpu.VMEM((8 + tt, LRU_WIDTH), F32), pltpu.VMEM((1, LRU_WIDTH), F32)],
        compiler_params=_params("arbitrary", "arbitrary"),
        name="inproj_lru",
    )(x, mod, g2, w_lru, w_rest, cos_t, sin_t, conv_w, conv_b, wab_bf16, bab, sp)


def _attn_kernel(q_ref, k_ref, v_ref, o_ref, kmean_ref, s_ref, mrun_ref, acc_ref, *, n_blk, n_pair):
    j = pl.program_id(1)
    blk = MOBA_BLOCK
    rows = 2 * blk
    kw = 2 * LANES

    @pl.when(j == 0)
    def _():
        for n in range(n_blk):
            for p in range(n_pair):
                kb = k_ref[n * blk:(n + 1) * blk, kw * p:kw * p + LANES]
                kmean_ref[n:n + 1, LANES * p:LANES * (p + 1)] = jnp.mean(kb.astype(F32), axis=0, keepdims=True)

    lane = lax.broadcasted_iota(jnp.int32, (blk, LANES), 1)
    bid = lax.broadcasted_iota(jnp.int32, (n_blk, rows), 0)
    valid = bid < j
    place = jnp.where(lax.broadcasted_iota(jnp.int32, (n_blk, LANES), 0)
                      == lax.broadcasted_iota(jnp.int32, (n_blk, LANES), 1), 1.0, 0.0).astype(BF16)
    own = pl.multiple_of(j * blk, blk)
    qpos = lax.broadcasted_iota(jnp.int32, (rows, blk), 0) % blk
    kpos = lax.broadcasted_iota(jnp.int32, (rows, blk), 1)
    causal = kpos <= qpos

    q_aug = []
    s_own = []
    for p in range(n_pair):
        q = q_ref[:, LANES * p:LANES * (p + 1)]
        zero = jnp.zeros_like(q)
        q2 = jnp.concatenate([jnp.where(lane < HEAD_DIM, q, zero),
                              jnp.where(lane >= HEAD_DIM, q, zero)], axis=0)
        kmean = kmean_ref[:, LANES * p:LANES * (p + 1)]
        km_hi = kmean.astype(BF16)
        km_lo = (kmean - km_hi.astype(F32)).astype(BF16)
        gw = jnp.where(valid, _dot_t(km_hi, q2) + _dot_t(km_lo, q2), NEG)
        sel = jnp.zeros(gw.shape, dtype=jnp.bool_)
        for _ in range(MOBA_TOPK):
            mx = jnp.max(gw, axis=0, keepdims=True)
            idx = jnp.min(jnp.where(gw == mx, bid, n_blk), axis=0, keepdims=True)
            pick = bid == idx
            sel = jnp.logical_or(sel, pick)
            gw = jnp.where(pick, -jnp.inf, gw)
        bias_t = jnp.where(jnp.logical_and(sel, valid), 0.0, NEG).astype(BF16)
        bias = lax.dot_general(bias_t, place, (((0,), (0,)), ((), ())), preferred_element_type=F32)
        q_aug.append(jnp.concatenate([q2, bias.astype(BF16)], axis=1))
        so = jnp.where(causal, _dot_t(q2, k_ref[pl.ds(own, blk), kw * p:kw * p + LANES]), NEG)
        s_own.append(so)
        mrun_ref[p] = jnp.maximum(so[:, :LANES], so[:, LANES:])

    n_two = (j + 1) // 2

    def walk(body):
        def pair_of(i, carry):
            body(2 * i)
            body(2 * i + 1)
            return carry
        lax.fori_loop(0, n_two // 2, pair_of, 0)

        @pl.when(n_two % 2 == 1)
        def _():
            body(n_two - 1)

    def scores(i):
        start = pl.multiple_of(i * (2 * blk), 2 * blk)
        for p in range(n_pair):
            sn = _dot_t(q_aug[p], k_ref[pl.ds(start, 2 * blk), kw * p:kw * (p + 1)])
            s_ref[p, i] = sn
            top = jnp.maximum(jnp.maximum(sn[:, :LANES], sn[:, LANES:2 * LANES]),
                              jnp.maximum(sn[:, 2 * LANES:3 * LANES], sn[:, 3 * LANES:]))
            mrun_ref[p] = jnp.maximum(mrun_ref[p], top)

    walk(scores)

    m = []
    for p in range(n_pair):
        mp = jnp.max(mrun_ref[p], axis=1, keepdims=True)
        m.append(mp)
        acc_ref[p] = _dot(jnp.exp2(s_own[p] - mp).astype(BF16), v_ref[pl.ds(own, blk), kw * p:kw * (p + 1)])

    def values(i):
        start = pl.multiple_of(i * (2 * blk), 2 * blk)
        for p in range(n_pair):
            pn = jnp.exp2(s_ref[p, i] - m[p]).astype(BF16)
            acc_ref[p] += _dot(pn, v_ref[pl.ds(start, 2 * blk), kw * p:kw * (p + 1)])

    walk(values)
    for p in range(n_pair):
        acc = acc_ref[p]
        out = acc[:, :LANES] / acc[:, LANES:]
        o_ref[:, LANES * p:LANES * (p + 1)] = jnp.where(lane < HEAD_DIM, out[:blk], out[blk:])


def _attention(q, kaug, vaug):
    b, s, w = q.shape
    n_blk = s // MOBA_BLOCK
    n_pair = w // LANES
    rows = 2 * MOBA_BLOCK
    kv_mode = pl.Buffered(2)
    return pl.pallas_call(
        functools.partial(_attn_kernel, n_blk=n_blk, n_pair=n_pair),
        grid=(b, n_blk),
        in_specs=[pl.BlockSpec((None, MOBA_BLOCK, w), lambda i, j: (i, j, 0)),
                  pl.BlockSpec((None, s, 2 * w), lambda i, j: (i, 0, 0), pipeline_mode=kv_mode),
                  pl.BlockSpec((None, s, 2 * w), lambda i, j: (i, 0, 0), pipeline_mode=kv_mode)],
        out_specs=pl.BlockSpec((None, MOBA_BLOCK, w), lambda i, j: (i, j, 0)),
        out_shape=jax.ShapeDtypeStruct((b, s, w), F32),
        scratch_shapes=[pltpu.VMEM((n_blk, w), F32),
                        pltpu.VMEM((n_pair, n_blk // 2, rows, 2 * MOBA_BLOCK), F32),
                        pltpu.VMEM((n_pair, rows, LANES), F32),
                        pltpu.VMEM((n_pair, rows, 2 * LANES), F32)],
        compiler_params=pltpu.CompilerParams(dimension_semantics=("arbitrary", "arbitrary"),
                                             vmem_limit_bytes=ATTN_VMEM_LIMIT),
        name="moba_attention",
    )(q, kaug, vaug)


def _lru_tile(xr, gate, cw_ref, cb_ref, wab_ref, bab_ref, sp_ref, o_ref, hist_ref, h_ref):
    lt = xr.shape[0]
    w = LRU_WIDTH
    hist_ref[8:8 + lt, :] = xr
    cw = cw_ref[...]
    n_tap = cw.shape[0]
    xc = cb_ref[...] + cw[n_tap - 1:n_tap] * xr
    for back in range(1, n_tap):
        xc = xc + cw[n_tap - 1 - back:n_tap - back] * hist_ref[8 - back:8 - back + lt, :]
    hist_ref[0:8, :] = xr[lt - 8:lt]

    ri = _sigmoid(_dot(xc.astype(BF16), wab_ref[...]) + bab_ref[...])
    r = ri[:, :w]
    i = ri[:, w:]
    log_a = (-LRU_C) * r * sp_ref[...]
    a = jnp.exp(log_a)
    expm1 = jnp.tanh(log_a) * (a * a + 1.0)
    u = jnp.sqrt(jnp.maximum(-expm1, 0.0)) * (i * xc)

    n_grp = lt // SUBLANES
    u = u.reshape(n_grp, SUBLANES, w)
    a = a.reshape(n_grp, SUBLANES, w)
    row = lax.broadcasted_iota(jnp.int32, (SUBLANES, w), 0)
    for k in range(3):
        d = 1 << k
        inside = row >= d
        u = u + jnp.where(inside, a, 0.0) * pltpu.roll(u, d, 1)
        a = a * jnp.where(inside, pltpu.roll(a, d, 1), 1.0)
    gg = _gelu(gate)
    carry = h_ref[...]
    for g in range(n_grp):
        rows = slice(g * SUBLANES, (g + 1) * SUBLANES)
        hg = u[g] + a[g] * carry
        o_ref[rows, :] = hg * gg[rows]
        carry = hg[SUBLANES - 1:SUBLANES]
    h_ref[...] = carry


S5_STEPS = 64


def _s5_kernel(u_ref, wb_ref, a_ref, wc_ref, d_ref, gw_ref, gb_ref, o_ref, xs_ref, bu_ref, st_ref):
    nb, lt, w = u_ref.shape
    ns = S5_NS
    rows = nb * lt

    @pl.when(pl.program_id(0) == 0)
    def _():
        xs_ref[...] = jnp.zeros_like(xs_ref)

    u = u_ref[...].reshape(rows, w)
    r = lax.broadcasted_iota(jnp.int32, (rows, rows), 0)
    c = lax.broadcasted_iota(jnp.int32, (rows, rows), 1)
    to_tm = jnp.where((r % nb) * lt + r // nb == c, 1.0, 0.0).astype(BF16)
    to_bt = jnp.where((r % lt) * nb + r // lt == c, 1.0, 0.0).astype(BF16)
    u_tm = _dot(to_tm, u.astype(BF16)).astype(BF16)
    bu_ref[...] = _dot(u_tm, wb_ref[...]).reshape(lt, nb, 2 * ns)
    ar = jnp.broadcast_to(a_ref[0:1, :], (nb, ns))
    ai = jnp.broadcast_to(a_ref[1:2, :], (nb, ns))

    def step(t, carry):
        re, im = carry
        b = bu_ref[t]
        re, im = ar * re - ai * im + b[:, :ns], ar * im + ai * re + b[:, ns:]
        st_ref[t] = jnp.concatenate([re, im], axis=1)
        return re, im

    x0 = xs_ref[...]
    carry = (x0[:, :ns], x0[:, ns:])
    for t in range(lt):
        carry = step(t, carry)
    re, im = carry
    xs_ref[...] = jnp.concatenate([re, im], axis=1)
    y_tm = _dot(st_ref[...].reshape(rows, 2 * ns).astype(BF16), wc_ref[...])
    y_hi = y_tm.astype(BF16)
    y_lo = (y_tm - y_hi.astype(F32)).astype(BF16)
    y_bt = _dot(to_bt, y_hi) + _dot(to_bt, y_lo)
    y = _gelu(y_bt + d_ref[...] * u)
    z = _sigmoid(_dot(y.astype(BF16), gw_ref[...]) + gb_ref[...])
    o_ref[...] = (y * z).reshape(nb, lt, w)


def _s5(u, wb_bf16, a1, wc_bf16, d_skip, glu_w_bf16, glu_b):
    b, s, w = u.shape
    lt = min(S5_STEPS, s)
    tok = pl.BlockSpec((b, lt, w), lambda i: (0, i, 0))
    full = lambda a: pl.BlockSpec(a.shape, lambda i: (0,) * a.ndim)
    return pl.pallas_call(
        _s5_kernel,
        grid=(s // lt,),
        in_specs=[tok, full(wb_bf16), full(a1), full(wc_bf16), full(d_skip), full(glu_w_bf16), full(glu_b)],
        out_specs=tok,
        out_shape=jax.ShapeDtypeStruct((b, s, w), F32),
        scratch_shapes=[pltpu.VMEM((b, 2 * S5_NS), F32), pltpu.VMEM((lt, b, 2 * S5_NS), F32),
                        pltpu.VMEM((lt, b, 2 * S5_NS), F32)],
        compiler_params=_params("arbitrary"),
        name="s5",
    )(u, wb_bf16, a1, wc_bf16, d_skip, glu_w_bf16, glu_b)


def _s5_tables(lam_re, lam_im, log_dt, b_re, b_im, c_re, c_im):
    g, p, h = b_re.shape
    dt = jnp.exp(log_dt)[:, None]
    mag = jnp.exp(lam_re * dt)
    ab_re, ab_im = mag * jnp.cos(lam_im * dt), mag * jnp.sin(lam_im * dt)
    den = lam_re * lam_re + lam_im * lam_im
    nr, ni = ab_re - 1.0, ab_im
    co_re = (nr * lam_re + ni * lam_im) / den
    co_im = (ni * lam_re - nr * lam_im) / den
    bb_re = co_re[..., None] * b_re - co_im[..., None] * b_im
    bb_im = co_re[..., None] * b_im + co_im[..., None] * b_re
    eye = jnp.eye(g, dtype=F32)
    wb = jnp.concatenate([jnp.einsum('gph,gk->ghkp', bb_re, eye).reshape(g * h, g * p),
                          jnp.einsum('gph,gk->ghkp', bb_im, eye).reshape(g * h, g * p)], axis=1)
    wc = jnp.concatenate([jnp.einsum('ghp,gk->kpgh', c_re, eye).reshape(g * p, g * h),
                          -jnp.einsum('ghp,gk->kpgh', c_im, eye).reshape(g * p, g * h)], axis=0)

    a1 = jnp.stack([ab_re.reshape(g * p), ab_im.reshape(g * p)])
    return wb.astype(BF16), a1, wc.astype(BF16)


def _mix_residual(oa_ref, ol_ref, os_ref, x_ref, mod, mg_ref, w_ref, g2_ref):
    mg = mg_ref[...]
    a, l = ATT_WIDTH, LRU_WIDTH
    o = (_dot(_rms(oa_ref[...], mg[:, :a]).astype(BF16), w_ref[:a, :])
         + _dot(_rms(ol_ref[...], mg[:, a:a + l]).astype(BF16), w_ref[a:a + l, :])
         + _dot(_rms(os_ref[...], mg[:, a + l:]).astype(BF16), w_ref[a + l:, :]))
    x = x_ref[...] + mod[2:3] * o
    h = _rms(x, g2_ref[...]) * (1.0 + mod[4:5]) + mod[3:4]
    return x, h


def _outproj_ffn_kernel(oa_ref, ol_ref, os_ref, x_ref, mod_ref, mg_ref, w_ref, g2_ref,
                        wg_ref, wu_ref, wd_ref, fg_ref, o_ref, *, final):
    mod = mod_ref[...]
    x, h = _mix_residual(oa_ref, ol_ref, os_ref, x_ref, mod, mg_ref, w_ref, g2_ref)
    hb = h.astype(BF16)
    g = _dot(hb, wg_ref[...])
    u = _dot(hb, wu_ref[...])
    f = _dot(((g * _sigmoid(g)) * u).astype(BF16), wd_ref[...])
    x = x + mod[5:6] * f
    o_ref[...] = _rms(x, fg_ref[...]) if final else x


def _outproj_moe_kernel(oa_ref, ol_ref, os_ref, x_ref, mod_ref, mg_ref, w_ref, g2_ref, rwt_ref,
                        x_out, h_out, ti_out, tw_out):
    x, h = _mix_residual(oa_ref, ol_ref, os_ref, x_ref, mod_ref[...], mg_ref, w_ref, g2_ref)
    x_out[...] = x
    h_out[...] = h
    hb = h.astype(BF16)
    h_lo = (h - hb.astype(F32)).astype(BF16)
    rwt = rwt_ref[...]
    rw_hi = rwt.astype(BF16)
    rw_lo = (rwt - rw_hi.astype(F32)).astype(BF16)
    logits = _dot_t(rw_hi, hb) + _dot_t(rw_lo, hb) + _dot_t(rw_hi, h_lo)
    eid = lax.broadcasted_iota(jnp.int32, logits.shape, 0)
    m1 = jnp.max(logits, axis=0, keepdims=True)
    i1 = jnp.min(jnp.where(logits == m1, eid, N_EXPERTS), axis=0, keepdims=True)
    rest_l = jnp.where(eid == i1, -jnp.inf, logits)
    m2 = jnp.max(rest_l, axis=0, keepdims=True)
    i2 = jnp.min(jnp.where(rest_l == m2, eid, N_EXPERTS), axis=0, keepdims=True)
    e2 = jnp.exp(m2 - m1)
    den = 1.0 + e2
    ti_out[...] = jnp.concatenate([i1, i2], axis=0)
    tw_out[...] = jnp.concatenate([1.0 / den, e2 / den], axis=0)


def _outproj_specs(x, tt):
    b, s, d = x.shape
    tok = lambda w: pl.BlockSpec((None, tt, w), lambda i, j: (i, j, 0))
    once = lambda r, c: pl.BlockSpec((r, c), lambda i, j: (0, 0), pipeline_mode=pl.Buffered(1))
    specs = [tok(ATT_WIDTH), tok(LRU_WIDTH), tok(S5_WIDTH), tok(d),
             pl.BlockSpec((None, 6, d), lambda i, j: (i, 0, 0)), once(1, d), once(d, d), once(1, d)]
    return tok, once, specs


def _outproj_ffn(o_att, o_lru, o_s5, x, mod, mix_gain, w_out_bf16, g2, wg, wu, wd, final_g):
    b, s, d = x.shape
    tt = min(TOK_TILE, s)
    dff = wg.shape[1]
    tok, once, specs = _outproj_specs(x, tt)
    fg = jnp.ones((d,), F32) if final_g is None else final_g
    return pl.pallas_call(
        functools.partial(_outproj_ffn_kernel, final=final_g is not None),
        grid=(b, s // tt),
        in_specs=specs + [once(d, dff), once(d, dff), once(dff, d), once(1, d)],
        out_specs=tok(d),
        out_shape=jax.ShapeDtypeStruct((b, s, d), F32),
        compiler_params=pltpu.CompilerParams(dimension_semantics=("arbitrary", "arbitrary"),
                                             vmem_limit_bytes=ATTN_VMEM_LIMIT),
        name="outproj_ffn",
    )(o_att, o_lru, o_s5, x, mod, mix_gain.reshape(1, d), w_out_bf16, g2.reshape(1, d), wg, wu, wd,
      fg.reshape(1, d))


def _outproj_moe(o_att, o_lru, o_s5, x, mod, mix_gain, w_out_bf16, g2, router_w):
    b, s, d = x.shape
    tt = min(TOK_TILE, s)
    tok, once, specs = _outproj_specs(x, tt)
    top = pl.BlockSpec((None, 2, tt), lambda i, j: (i, 0, j))
    x_new, h, ti, tw = pl.pallas_call(
        _outproj_moe_kernel,
        grid=(b, s // tt),
        in_specs=specs + [once(N_EXPERTS, d)],
        out_specs=[tok(d), tok(d), top, top],
        out_shape=[jax.ShapeDtypeStruct((b, s, d), F32), jax.ShapeDtypeStruct((b, s, d), F32),
                   jax.ShapeDtypeStruct((b, 2, s), jnp.int32), jax.ShapeDtypeStruct((b, 2, s), F32)],
        compiler_params=_params("arbitrary", "arbitrary"),
        name="outproj_moe",
    )(o_att, o_lru, o_s5, x, mod, mix_gain.reshape(1, d), w_out_bf16, g2.reshape(1, d), router_w.T)
    return x_new, h, jnp.swapaxes(ti, 1, 2), jnp.swapaxes(tw, 1, 2)


def _rope_tables(positions):
    half = HEAD_DIM // 2
    inv = 10000.0 ** (-jnp.arange(0, HEAD_DIM, 2, dtype=F32) / HEAD_DIM)
    ang = positions.astype(F32)[..., None] * inv
    cos, sin = jnp.cos(ang), jnp.sin(ang)
    reps = LANES // HEAD_DIM
    cos_t = jnp.tile(jnp.concatenate([cos, cos], axis=-1), (1, 1, reps))
    sin_t = jnp.tile(jnp.concatenate([-sin, sin], axis=-1), (1, 1, reps))
    return cos_t, sin_t


def _block_diag(w):
    h, i, j = w.shape
    eye = jnp.eye(h, dtype=w.dtype)
    return jnp.einsum('hij,hk->hikj', w, eye).reshape(h * i, h * j)


def _moe_plan(ti):
    t = ti.shape[0]
    tm = MOE_TILE
    n_tiles = (2 * t) // tm + N_EXPERTS
    onehot = jnp.any(ti[:, :, None] == jnp.arange(N_EXPERTS, dtype=jnp.int32), axis=1)
    pos = jnp.cumsum(onehot.astype(jnp.int32), axis=0) - 1
    counts = pos[-1] + 1
    padded = ((counts + tm - 1) // tm) * tm
    ends = jnp.cumsum(padded)
    starts = ends - padded
    dest = jnp.take_along_axis(starts[None, :] + pos, ti, axis=1)
    tile_start = jnp.arange(n_tiles, dtype=jnp.int32) * tm
    tile_expert = jnp.minimum(jnp.sum((tile_start[:, None] >= ends[None, :]).astype(jnp.int32), axis=1),
                              N_EXPERTS - 1)
    n_active = ends[-1] // tm
    pad_tile = jnp.where(padded > counts, ends // tm - 1, -1)
    tail_tile = n_tiles - 1 - jnp.arange(N_EXPERTS, dtype=jnp.int32)
    zero_tiles = jnp.concatenate([pad_tile, jnp.where(tail_tile >= n_active, tail_tile, -1)]).astype(jnp.int32)
    return dest.astype(jnp.int32), tile_expert, n_active.astype(jnp.int32).reshape(1), zero_tiles


def _dispatch_kernel(zt_ref, dest_ref, h_hbm, xs_hbm, hbuf, zbuf, lsem, ssem, zsem):
    i = pl.program_id(0)
    n = pl.num_programs(0)
    tt = hbuf.shape[1]
    tm = zbuf.shape[0]

    def load(step, slot):
        start = pl.multiple_of(step * tt, tt)
        return pltpu.make_async_copy(h_hbm.at[pl.ds(start, tt), :], hbuf.at[slot], lsem.at[slot])

    def wait_rows(slot):
        for _ in range(2):
            pltpu.make_async_copy(hbuf.at[slot], xs_hbm.at[pl.ds(0, tt), :], ssem.at[slot]).wait()

    @pl.when(i == 0)
    def _():
        load(0, 0).start()
        zbuf[...] = jnp.zeros_like(zbuf)

        def clear(k):
            start = pl.multiple_of(zt_ref[k] * tm, tm)
            return pltpu.make_async_copy(zbuf, xs_hbm.at[pl.ds(start, tm), :], zsem)

        for k in range(zt_ref.shape[0]):
            @pl.when(zt_ref[k] >= 0)
            def _():
                clear(k).start()
        for k in range(zt_ref.shape[0]):
            @pl.when(zt_ref[k] >= 0)
            def _():
                clear(k).wait()

    for slot in range(2):
        @pl.when(i % 2 == slot)
        def _():
            @pl.when(i >= 1)
            def _():
                wait_rows(1 - slot)

            @pl.when(i + 1 < n)
            def _():
                load(i + 1, 1 - slot).start()

            load(i, slot).wait()
            for r in range(tt):
                for choice in range(2):
                    t = dest_ref[0, 0, 2 * r + choice]
                    pltpu.make_async_copy(hbuf.at[slot, pl.ds(r, 1), :], xs_hbm.at[pl.ds(t, 1), :],
                                          ssem.at[slot]).start(priority=choice)

            @pl.when(i == n - 1)
            def _():
                wait_rows(slot)


def _moe_dispatch(h2d, dest, zero_tiles, n_pad):
    t, d = h2d.shape
    tt = min(TOK_TILE, t)
    grid_spec = pltpu.PrefetchScalarGridSpec(
        num_scalar_prefetch=1,
        grid=(t // tt,),
        in_specs=[pl.BlockSpec((1, 1, 2 * tt), lambda i, zt: (i, 0, 0), memory_space=pltpu.SMEM),
                  pl.BlockSpec(memory_space=pl.ANY)],
        out_specs=pl.BlockSpec(memory_space=pl.ANY),
        scratch_shapes=[pltpu.VMEM((2, tt, d), F32), pltpu.VMEM((MOE_TILE, d), F32),
                        pltpu.SemaphoreType.DMA((2,)), pltpu.SemaphoreType.DMA((2,)),
                        pltpu.SemaphoreType.DMA(())])
    return pl.pallas_call(
        _dispatch_kernel,
        grid_spec=grid_spec,
        out_shape=jax.ShapeDtypeStruct((n_pad, d), F32),
        compiler_params=_params("arbitrary"),
        name="moe_dispatch",
    )(zero_tiles, dest.reshape(t // tt, 1, 2 * tt), h2d)


def _experts_kernel(te_ref, na_ref, x_ref, wg_ref, wu_ref, wd_ref, y_ref):
    active = pl.program_id(0) < na_ref[0]

    @pl.when(active)
    def _():
        x = x_ref[...].astype(BF16)
        g = _dot(x, wg_ref[...])
        u = _dot(x, wu_ref[...])
        y_ref[...] = _dot(((g * _sigmoid(g)) * u).astype(BF16), wd_ref[...])

    @pl.when(jnp.logical_not(active))
    def _():
        y_ref[...] = jnp.zeros_like(y_ref)


def _moe_experts(xs, tile_expert, n_active, wg, wu, wd):
    n_pad, d = xs.shape
    tm = MOE_TILE
    dff = wg.shape[2]
    once = pl.Buffered(1)
    grid_spec = pltpu.PrefetchScalarGridSpec(
        num_scalar_prefetch=2,
        grid=(n_pad // tm,),
        in_specs=[pl.BlockSpec((tm, d), lambda i, te, na: (i, 0)),
                  pl.BlockSpec((None, d, dff), lambda i, te, na: (te[i], 0, 0), pipeline_mode=once),
                  pl.BlockSpec((None, d, dff), lambda i, te, na: (te[i], 0, 0), pipeline_mode=once),
                  pl.BlockSpec((None, dff, d), lambda i, te, na: (te[i], 0, 0), pipeline_mode=once)],
        out_specs=pl.BlockSpec((tm, d), lambda i, te, na: (i, 0)))
    return pl.pallas_call(
        _experts_kernel,
        grid_spec=grid_spec,
        out_shape=jax.ShapeDtypeStruct((n_pad, d), F32),
        compiler_params=pltpu.CompilerParams(dimension_semantics=("arbitrary",),
                                             vmem_limit_bytes=ATTN_VMEM_LIMIT),
        name="moe_experts",
    )(tile_expert, n_active, xs, wg, wu, wd)


def _combine_kernel(dcur_ref, dnext_ref, ys_hbm, x_ref, tw_ref, mod_ref, g_ref, o_ref, buf, sem, *, final):
    i = pl.program_id(0)
    n = pl.num_programs(0)
    tt = x_ref.shape[0]

    def gather(d_ref, slot):
        for r in range(tt):
            for choice in range(2):
                t = d_ref[0, 0, 2 * r + choice]
                pltpu.make_async_copy(ys_hbm.at[pl.ds(t, 1), :], buf.at[slot, choice, pl.ds(r, 1), :],
                                      sem.at[slot]).start(priority=choice)

    @pl.when(i == 0)
    def _():
        gather(dcur_ref, 0)

    slot = i % 2
    for nxt in range(2):
        @pl.when(jnp.logical_and(i + 1 < n, slot == 1 - nxt))
        def _():
            gather(dnext_ref, nxt)

    for choice in range(2):
        pltpu.make_async_copy(ys_hbm.at[pl.ds(0, tt), :], buf.at[slot, choice], sem.at[slot]).wait()
    tw = tw_ref[...]
    y = tw[:, 0:1] * buf[slot, 0] + tw[:, 1:2] * buf[slot, 1]
    x = x_ref[...] + mod_ref[...][5:6] * y
    o_ref[...] = _rms(x, g_ref[...]) if final else x


def _moe_combine(x, ys, dest, tw, mod, final_g):
    b, s, d = x.shape
    t = b * s
    tt = min(TOK_TILE, s)
    n = t // tt
    per_b = s // tt
    dest3 = dest.reshape(n, 1, 2 * tt)
    return pl.pallas_call(
        functools.partial(_combine_kernel, final=final_g is not None),
        grid=(n,),
        in_specs=[pl.BlockSpec((1, 1, 2 * tt), lambda i: (i, 0, 0), memory_space=pltpu.SMEM),
                  pl.BlockSpec((1, 1, 2 * tt), lambda i: (jnp.minimum(i + 1, n - 1), 0, 0), memory_space=pltpu.SMEM),
                  pl.BlockSpec(memory_space=pl.ANY),
                  pl.BlockSpec((tt, d), lambda i: (i, 0)),
                  pl.BlockSpec((tt, 2), lambda i: (i, 0)),
                  pl.BlockSpec((None, 6, d), lambda i: (i // per_b, 0, 0)),
                  pl.BlockSpec((1, d), lambda i: (0, 0))],
        out_specs=pl.BlockSpec((tt, d), lambda i: (i, 0)),
        out_shape=jax.ShapeDtypeStruct((t, d), F32),
        scratch_shapes=[pltpu.VMEM((2, 2, tt, d), F32), pltpu.SemaphoreType.DMA((2,))],
        compiler_params=_params("arbitrary"),
        name="moe_combine",
    )(dest3, dest3, ys, x.reshape(t, d), tw, mod,
      (jnp.ones((d,), F32) if final_g is None else final_g).reshape(1, d)).reshape(b, s, d)


def kernel(x, c, positions, w_in, lru_conv_w, lru_conv_b, lru_w_a, lru_b_a, lru_w_x, lru_b_x, lru_lambda,
           s5_lambda_re, s5_lambda_im, s5_log_dt, s5_b_re, s5_b_im, s5_c_re, s5_c_im, s5_d, s5_glu_w,
           s5_glu_b, mix_gain, w_out, norm1_g, norm2_g, ada_w, ada_b, ffn_w_gate, ffn_w_up, ffn_w_down,
           router_w, moe_w_gate, moe_w_up, moe_w_down, final_g):
    b, s, d = x.shape
    depth = w_in.shape[0]
    cos_t, sin_t = _rope_tables(positions)
    mods = _adaln_mod(c, ada_w, ada_b).reshape(depth, b, 6, d)
    for l in range(depth):
        mod = mods[l]
        wab = jnp.concatenate([_block_diag(lru_w_a[l]), _block_diag(lru_w_x[l])], axis=1).astype(BF16)
        bab = jnp.concatenate([lru_b_a[l], lru_b_x[l]]).reshape(1, -1)
        sp = jax.nn.softplus(-lru_lambda[l]).reshape(1, -1)
        q, kaug, vaug, o_lru, u = _inproj(x, mod, norm1_g[l], w_in[l].astype(BF16), cos_t, sin_t,
                                          lru_conv_w[l], lru_conv_b[l].reshape(1, -1), wab, bab, sp)
        o_att = _attention(q, kaug, vaug)
        wb, a1, wc = _s5_tables(s5_lambda_re[l], s5_lambda_im[l], s5_log_dt[l], s5_b_re[l], s5_b_im[l],
                                s5_c_re[l], s5_c_im[l])
        o_s5 = _s5(u, wb, a1, wc, s5_d[l].reshape(1, -1), s5_glu_w[l].astype(BF16), s5_glu_b[l].reshape(1, -1))
        fg = final_g if l == depth - 1 else None
        e = l // 2
        if l % 2 == 0:
            x = _outproj_ffn(o_att, o_lru, o_s5, x, mod, mix_gain[l], w_out[l].astype(BF16), norm2_g[l],
                             ffn_w_gate[e].astype(BF16), ffn_w_up[e].astype(BF16), ffn_w_down[e].astype(BF16), fg)
        else:
            x, h, ti, tw = _outproj_moe(o_att, o_lru, o_s5, x, mod, mix_gain[l], w_out[l].astype(BF16),
                                        norm2_g[l], router_w[e])
            dest, tile_expert, n_active, zero_tiles = _moe_plan(ti.reshape(b * s, 2))
            xs = _moe_dispatch(h.reshape(b * s, d), dest, zero_tiles, tile_expert.shape[0] * MOE_TILE)
            ys = _moe_experts(xs, tile_expert, n_active, moe_w_gate[e].astype(BF16), moe_w_up[e].astype(BF16),
                              moe_w_down[e].astype(BF16))
            x = _moe_combine(x, ys, dest, tw.reshape(b * s, 2), mod, fg)
    return x
```
